```python
import jax
import jax.numpy as jnp
from jax import lax
import numpy as np


D_MODEL = 1024
BATCH = 4
SEQ = 4096
DEPTH = 4

CHUNK = 64
HEAD_DIM = 64
D_MIX = D_MODEL
SB_HEADS = 4
SB_WIDTH = SB_HEADS * HEAD_DIM
SB_QBLOCK = 128
CA_HEADS = 4
CA_WIDTH = CA_HEADS * HEAD_DIM
CA_LEFT_CHUNKS = 8
CA_BAND = (CA_LEFT_CHUNKS + 1) * CHUNK
REL_CLIP = 256
REL_TABLE = REL_CLIP + CHUNK
RW_WIDTH = D_MIX - SB_WIDTH - CA_WIDTH
RW_HEADS = RW_WIDTH // HEAD_DIM
D_DECAY_LORA = max(32, int(round(1.8 * D_MODEL ** 0.5 / 32)) * 32)
D_AAA_LORA = max(32, int(round(1.8 * D_MODEL ** 0.5 / 32)) * 32)
D_MV_LORA = max(32, int(round(1.3 * D_MODEL ** 0.5 / 32)) * 32)
D_GATE_LORA = max(32, int(round(0.6 * D_MODEL ** 0.8 / 32)) * 32)
RW_SHIFT_COLS = 3 * RW_WIDTH + D_DECAY_LORA + D_AAA_LORA + D_GATE_LORA
RW_SPLITS = [RW_WIDTH, RW_WIDTH + D_DECAY_LORA, 2 * RW_WIDTH + D_DECAY_LORA,
             3 * RW_WIDTH + D_DECAY_LORA, 3 * RW_WIDTH + D_DECAY_LORA + D_AAA_LORA]
PROJ_SPLITS = [SB_WIDTH, 2 * SB_WIDTH, 3 * SB_WIDTH, 3 * SB_WIDTH + CA_WIDTH,
               3 * SB_WIDTH + 2 * CA_WIDTH, 3 * SB_WIDTH + 3 * CA_WIDTH]
N_IN = PROJ_SPLITS[-1] + RW_SHIFT_COLS
D_FF = 4 * D_MODEL
RMS_EPS = 1e-5
GN_EPS = 64e-5

kernel_name = 'hybrid_stickbreak_chunkrel_rwkv7_encoder'


def rmsnorm(x, g):
    x32 = x.astype(jnp.float32)
    y = x32 * lax.rsqrt(jnp.mean(x32 * x32, axis=-1, keepdims=True) + RMS_EPS)
    return (y * g.astype(jnp.float32)).astype(x.dtype)


def token_shift(x):
    return jnp.pad(x[:, :-1], ((0, 0), (1, 0), (0, 0)))


def stick_breaking_attention(q, k, v):
    B, S, H, d = q.shape
    nq = S // SB_QBLOCK
    scale = d ** -0.5
    k32 = k.astype(jnp.float32)
    v32 = v.astype(jnp.float32)
    key_pos = jnp.arange(S)
    qb = q.astype(jnp.float32).reshape(B, nq, SB_QBLOCK, H, d).transpose(1, 0, 2, 3, 4)

    def block(args):
        qi, i = args
        z = jnp.einsum('bqhd,bshd->bhqs', qi, k32) * scale
        q_pos = i * SB_QBLOCK + jnp.arange(SB_QBLOCK)
        strict = (key_pos[None, :] < q_pos[:, None])[None, None]
        log_1m_beta = jnp.where(strict, jax.nn.log_sigmoid(-z), 0.0)
        after = lax.cumsum(log_1m_beta, axis=3, reverse=True) - log_1m_beta
        weights = jnp.where(strict, jnp.exp(jax.nn.log_sigmoid(z) + after), 0.0)
        return jnp.einsum('bhqs,bshd->bqhd', weights, v32)

    out = lax.map(block, (qb, jnp.arange(nq)))
    return out.transpose(1, 0, 2, 3, 4).reshape(B, S, H * d)


def chunked_relpos_attention(q, k, v, rel_table):
    B, S, H, d = q.shape
    nc = S // CHUNK
    pad = CA_LEFT_CHUNKS * CHUNK
    qc = q.astype(jnp.float32).reshape(B, nc, CHUNK, H, d)
    kp = jnp.pad(k.astype(jnp.float32), ((0, 0), (pad, 0), (0, 0), (0, 0))).reshape(B, nc + CA_LEFT_CHUNKS, CHUNK, H, d)
    vp = jnp.pad(v.astype(jnp.float32), ((0, 0), (pad, 0), (0, 0), (0, 0))).reshape(B, nc + CA_LEFT_CHUNKS, CHUNK, H, d)
    kband = jnp.concatenate([kp[:, o:o + nc] for o in range(CA_LEFT_CHUNKS + 1)], axis=2)
    vband = jnp.concatenate([vp[:, o:o + nc] for o in range(CA_LEFT_CHUNKS + 1)], axis=2)
    s = jnp.einsum('bcqhd,bckhd->bhcqk', qc, kband) * (d ** -0.5)
    qi = np.arange(CHUNK)[:, None]
    kj = np.arange(CA_BAND)[None, :]
    rel = kj - pad - qi
    idx = np.clip(rel, -REL_CLIP, CHUNK - 1) + REL_CLIP
    bias = rel_table.astype(jnp.float32)[:, idx]
    key_abs = np.arange(nc)[:, None] * CHUNK + np.arange(CA_BAND)[None, :] - pad
    valid = (key_abs >= 0)[None, None, :, None, :]
    s = jnp.where(valid, s + bias[None, :, None], -jnp.inf)
    p = jax.nn.softmax(s, axis=-1)
    out = jnp.einsum('bhcqk,bckhd->bcqhd', p, vband)
    return out.reshape(B, S, H * d)


def wkv7_scan(r, w, k, v, a, b):
    B, S, H, N = r.shape

    def step(state, inp):
        r_t, w_t, k_t, v_t, a_t, b_t = inp
        sa = jnp.einsum('bhvk,bhk->bhv', state, a_t)
        state = (state * w_t[:, :, None, :] + sa[..., None] * b_t[:, :, None, :]
                 + v_t[..., None] * k_t[:, :, None, :])
        return state, jnp.einsum('bhvk,bhk->bhv', state, r_t)

    xs = tuple(jnp.moveaxis(t, 1, 0) for t in (r, w, k, v, a, b))
    _, y = lax.scan(step, jnp.zeros((B, H, N, N), jnp.float32), xs)
    return jnp.moveaxis(y, 0, 1)


def rwkv7_time_mix(cols, vd, v_first, mu, w0, w2, a0, a2, v0, v2, g2, k_k, k_a, r_k, ln_w, ln_b):
    B, S, _ = cols.shape
    f32 = jnp.float32
    c = cols.astype(f32)
    c = c + (token_shift(c) - c) * mu.astype(f32)
    r, w_lo, k, v, a_lo, g_lo = jnp.split(c, RW_SPLITS, axis=-1)
    log_w = -jax.nn.softplus(-(w0.astype(f32) + jnp.tanh(w_lo) @ w2.astype(f32))) - 0.5
    decay = jnp.exp(-jnp.exp(log_w))
    if vd is None:
        v_first = v
    else:
        v = v + (v_first - v) * jax.nn.sigmoid(v0.astype(f32) + vd.astype(f32) @ v2.astype(f32))
    a = jax.nn.sigmoid(a0.astype(f32) + a_lo @ a2.astype(f32))
    g = jax.nn.sigmoid(g_lo) @ g2.astype(f32)

    def heads(t):
        return t.reshape(B, S, RW_HEADS, HEAD_DIM)

    kk = heads(k * k_k.astype(f32))
    kk = kk / jnp.maximum(jnp.sqrt(jnp.sum(kk * kk, axis=-1, keepdims=True)), 1e-12)
    k = k * (1.0 + (a - 1.0) * k_a.astype(f32))
    rh, kh, vh, ah = heads(r), heads(k), heads(v), heads(a)
    y = wkv7_scan(rh, heads(decay), kh, vh, -kk, kk * ah)
    mean = jnp.mean(y, axis=-1, keepdims=True)
    var = jnp.mean(jnp.square(y - mean), axis=-1, keepdims=True)
    y = ((y - mean) * lax.rsqrt(var + GN_EPS)).reshape(B, S, RW_WIDTH) * ln_w.astype(f32) + ln_b.astype(f32)
    bonus = jnp.sum(rh * kh * r_k.astype(f32).reshape(RW_HEADS, HEAD_DIM), axis=-1, keepdims=True) * vh
    y = (y + bonus.reshape(B, S, RW_WIDTH)) * g
    return y, v_first


def setup_inputs(seed: int = 0) -> dict:
    key = jax.random.key(seed)
    ks = jax.random.split(key, 26)
    f32 = jnp.float32
    L = DEPTH

    def nrm(k, shape, scale):
        return jax.random.normal(k, shape, f32) * scale

    def gain(k, shape):
        return 1.0 + 0.02 * jax.random.normal(k, shape, f32)

    return {
        'x': nrm(ks[0], (BATCH, SEQ, D_MODEL), 1.0),
        'norm_mix_g': gain(ks[1], (L, D_MODEL)),
        'w_in': nrm(ks[2], (L, D_MODEL, N_IN), D_MODEL ** -0.5),
        'w_vmix_down': nrm(ks[3], (L - 1, D_MODEL, D_MV_LORA), D_MODEL ** -0.5),
        'sb_out_g': gain(ks[4], (L, SB_WIDTH)),
        'ca_rel_bias': nrm(ks[5], (L, CA_HEADS, REL_TABLE), 0.1),
        'ca_out_g': gain(ks[6], (L, CA_WIDTH)),
        'rw_mu': jax.random.uniform(ks[7], (L, RW_SHIFT_COLS), f32),
        'rw_w0': jax.random.uniform(ks[8], (L, RW_WIDTH), f32, -6.5, -1.5),
        'rw_w2': nrm(ks[9], (L, D_DECAY_LORA, RW_WIDTH), 0.1 * D_DECAY_LORA ** -0.5),
        'rw_a0': nrm(ks[10], (L, RW_WIDTH), 0.1),
        'rw_a2': nrm(ks[11], (L, D_AAA_LORA, RW_WIDTH), 0.1 * D_AAA_LORA ** -0.5),
        'rw_v0': 1.0 + nrm(ks[12], (L - 1, RW_WIDTH), 0.1),
        'rw_v2': nrm(ks[13], (L - 1, D_MV_LORA, RW_WIDTH), 0.1 * D_MV_LORA ** -0.5),
        'rw_g2': nrm(ks[14], (L, D_GATE_LORA, RW_WIDTH), D_GATE_LORA ** -0.5),
        'rw_k_k': 0.85 + nrm(ks[15], (L, RW_WIDTH), 0.02),
        'rw_k_a': 1.0 + nrm(ks[16], (L, RW_WIDTH), 0.02),
        'rw_r_k': nrm(ks[17], (L, RW_WIDTH), 0.1),
        'rw_ln_w': gain(ks[18], (L, RW_WIDTH)),
        'rw_ln_b': nrm(ks[19], (L, RW_WIDTH), 0.02),
        'w_out': nrm(ks[20], (L, D_MIX, D_MODEL), D_MIX ** -0.5),
        'norm_ffn_g': gain(ks[21], (L, D_MODEL)),
        'w_ff_in': nrm(ks[22], (L, D_MODEL, D_FF), D_MODEL ** -0.5),
        'w_ff_out': nrm(ks[23], (L, D_FF, D_MODEL), D_FF ** -0.5),
        'norm_final_g': gain(ks[24], (D_MODEL,)),
    }


def reference(x, norm_mix_g, w_in, w_vmix_down, sb_out_g, ca_rel_bias, ca_out_g, rw_mu, rw_w0, rw_w2,
              rw_a0, rw_a2, rw_v0, rw_v2, rw_g2, rw_k_k, rw_k_a, rw_r_k, rw_ln_w, rw_ln_b, w_out,
              norm_ffn_g, w_ff_in, w_ff_out, norm_final_g):
    B, S, _ = x.shape
    v_first = None
    for l in range(DEPTH):
        h = rmsnorm(x, norm_mix_g[l])
        w_proj = w_in[l] if l == 0 else jnp.concatenate([w_in[l], w_vmix_down[l - 1]], axis=1)
        proj = jnp.einsum('bsd,dn->bsn', h, w_proj)
        sb_q, sb_k, sb_v, ca_q, ca_k, ca_v, rest = jnp.split(proj, PROJ_SPLITS, axis=-1)
        rw_cols = rest[..., :RW_SHIFT_COLS]
        vd = None if l == 0 else rest[..., RW_SHIFT_COLS:]

        def sb_heads(t):
            return t.reshape(B, S, SB_HEADS, HEAD_DIM)

        def ca_heads(t):
            return t.reshape(B, S, CA_HEADS, HEAD_DIM)

        sb = rmsnorm(stick_breaking_attention(sb_heads(sb_q), sb_heads(sb_k), sb_heads(sb_v)), sb_out_g[l])
        ca = rmsnorm(chunked_relpos_attention(ca_heads(ca_q), ca_heads(ca_k), ca_heads(ca_v), ca_rel_bias[l]), ca_out_g[l])
        rw, v_first = rwkv7_time_mix(
            rw_cols, vd, v_first, rw_mu[l], rw_w0[l], rw_w2[l], rw_a0[l], rw_a2[l],
            None if l == 0 else rw_v0[l - 1], None if l == 0 else rw_v2[l - 1],
            rw_g2[l], rw_k_k[l], rw_k_a[l], rw_r_k[l], rw_ln_w[l], rw_ln_b[l])
        mix = jnp.concatenate([sb, ca, rw], axis=-1).astype(x.dtype)
        x = x + jnp.einsum('bsm,md->bsd', mix, w_out[l])
        h2 = rmsnorm(x, norm_ffn_g[l])
        f = jnp.square(jax.nn.relu(jnp.einsum('bsd,df->bsf', h2, w_ff_in[l])))
        x = x + jnp.einsum('bsf,fd->bsd', f, w_ff_out[l])
    return rmsnorm(x, norm_final_g)
```

```python
import functools

import jax
import jax.numpy as jnp
from jax import lax
from jax.experimental import pallas as pl
from jax.experimental.pallas import tpu as pltpu

F32 = jnp.float32
BF16 = jnp.bfloat16

LANES = 128
HEAD_DIM = 64
CHUNK = 64
SB_WIDTH = 256
CA_WIDTH = 256
RW_WIDTH = 512
CA_LEFT_CHUNKS = 8
REL_CLIP = 256
D_DECAY_LORA = 64
D_AAA_LORA = 64
D_GATE_LORA = 160
D_MV_LORA = 32
RMS_EPS = 1e-5
GN_EPS = 64e-5
MASK_VALUE = -1e30

COL_SB = 0
COL_CA = 3 * SB_WIDTH
COL_RW = COL_CA + 3 * CA_WIDTH
LORA_WA = 2 * HEAD_DIM
LORA_GV = 256
COL_LORA = COL_RW + 3 * RW_WIDTH
N_PROJ = COL_LORA + LORA_WA + LORA_GV

VMEM_LIMIT = 56 * 1024 * 1024


def _cparams(*sem):
    return pltpu.CompilerParams(dimension_semantics=sem, vmem_limit_bytes=VMEM_LIMIT)


def _dot(a, b):
    return jnp.dot(a, b, preferred_element_type=F32)


def _dot_nt(a, b):
    return lax.dot_general(a, b, (((1,), (1,)), ((), ())), preferred_element_type=F32)


def _dot_tn(a, b):
    return lax.dot_general(a, b, (((0,), (0,)), ((), ())), preferred_element_type=F32)


def _split_dot(x, m):
    hi = x.astype(BF16)
    lo = (x - hi.astype(F32)).astype(BF16)
    return _dot(hi, m) + _dot(lo, m)


def _split_dot_left(m, x):
    hi = x.astype(BF16)
    lo = (x - hi.astype(F32)).astype(BF16)
    return _dot(m, hi) + _dot(m, lo)


def _softplus(x):
    return jnp.maximum(x, 0.0) + jnp.log1p(jnp.exp(-jnp.abs(x)))


def _sigmoid(x):
    return 1.0 / (1.0 + jnp.exp(-x))


def _rms(x, g):
    return x * lax.rsqrt(jnp.mean(x * x, axis=-1, keepdims=True) + RMS_EPS) * g


def _inproj_kernel(x_ref, g_ref, w_ref, o_ref):
    h = _rms(x_ref[...], g_ref[...]).astype(BF16)
    o_ref[...] = _dot(h, w_ref[...])


def _inproj(x2d, g, w, tm):
    T, D = x2d.shape
    N = w.shape[1]
    return pl.pallas_call(
        _inproj_kernel,
        grid=(T // tm,),
        in_specs=[
            pl.BlockSpec((tm, D), lambda i: (i, 0)),
            pl.BlockSpec((1, D), lambda i: (0, 0)),
            pl.BlockSpec((D, N), lambda i: (0, 0)),
        ],
        out_specs=pl.BlockSpec((tm, N), lambda i: (i, 0)),
        out_shape=jax.ShapeDtypeStruct((T, N), F32),
        compiler_params=_cparams("parallel"),
        name="inproj",
    )(x2d, g, w)


def _sb_kernel(q_ref, k_ref, v_ref, tri_ref, o_ref, acc_ref, car_ref, *, tq):
    i = pl.program_id(2)
    lane = lax.broadcasted_iota(jnp.int32, (tq, LANES), 1)
    head0 = lane < HEAD_DIM
    q = q_ref[...] * (HEAD_DIM ** -0.5)
    qh = (jnp.where(head0, q, 0.0).astype(BF16), jnp.where(head0, 0.0, q).astype(BF16))
    tri = tri_ref[...]
    strict = (lax.broadcasted_iota(jnp.int32, (tq, tq), 1)
              < lax.broadcasted_iota(jnp.int32, (tq, tq), 0))
    acc_ref[...] = jnp.zeros_like(acc_ref)
    car_ref[...] = jnp.zeros_like(car_ref)

    def tile(kb, diag):
        start = pl.multiple_of(kb * tq, tq)
        kblk = k_ref[pl.ds(start, tq), :].astype(BF16)
        vblk = v_ref[pl.ds(start, tq), :].astype(BF16)
        for h in range(2):
            z = _dot_nt(qh[h], kblk)
            sp = _softplus(z)
            log_beta = z - sp
            log_1m = -sp
            if diag:
                log_1m = jnp.where(strict, log_1m, 0.0)
            after = _split_dot(log_1m, tri)
            carry = car_ref[h]
            w = jnp.exp(log_beta + after + carry)
            if diag:
                w = jnp.where(strict, w, 0.0)
            acc_ref[h] += _dot(w.astype(BF16), vblk)
            car_ref[h] = carry + after[:, :1] + log_1m[:, :1]

    tile(i, True)

    def body(j, c):
        tile(i - 1 - j, False)
        return c

    lax.fori_loop(0, i, body, 0)
    o_ref[...] = jnp.where(head0, acc_ref[0], acc_ref[1])


def _sb_attention(proj, tri, B, S, tq):
    T = B * S
    nq = S // tq
    kern = functools.partial(_sb_kernel, tq=tq)
    qcol = COL_SB // LANES
    kcol = (COL_SB + SB_WIDTH) // LANES
    vcol = (COL_SB + 2 * SB_WIDTH) // LANES
    return pl.pallas_call(
        kern,
        grid=(B, SB_WIDTH // LANES, nq),
        in_specs=[
            pl.BlockSpec((tq, LANES), lambda b, p, i: (b * nq + i, qcol + p)),
            pl.BlockSpec((S, LANES), lambda b, p, i: (b, kcol + p)),
            pl.BlockSpec((S, LANES), lambda b, p, i: (b, vcol + p)),
            pl.BlockSpec((tq, tq), lambda b, p, i: (0, 0)),
        ],
        out_specs=pl.BlockSpec((tq, LANES), lambda b, p, i: (b * nq + i, p)),
        out_shape=jax.ShapeDtypeStruct((T, SB_WIDTH), F32),
        scratch_shapes=[pltpu.VMEM((2, tq, LANES), F32), pltpu.VMEM((2, tq, 1), F32)],
        compiler_params=_cparams("parallel", "parallel", "arbitrary"),
        name="sb_attention",
    )(proj, proj, proj, tri)


def _ca_kernel(q_ref, k0_ref, k1_ref, k2_ref, v0_ref, v1_ref, v2_ref, rel_ref, o_ref, bias_ref,
               *, tq):
    i = pl.program_id(2)
    nk = 3 * tq
    left = CA_LEFT_CHUNKS * CHUNK

    @pl.when(i == 0)
    def _():
        qi = lax.broadcasted_iota(jnp.int32, (tq, nk), 0)
        kj = lax.broadcasted_iota(jnp.int32, (tq, nk), 1)
        qc = qi // CHUNK
        kc = kj // CHUNK
        band = (kc >= qc) & (kc <= qc + CA_LEFT_CHUNKS)
        for h in range(2):
            row = jnp.broadcast_to(rel_ref[h], (tq, rel_ref.shape[-1]))
            toep = pltpu.roll(row, rel_ref.shape[-1] - tq, 1, stride=1, stride_axis=0)
            bias_ref[h] = jnp.where(band, toep[:, :nk], MASK_VALUE)

    lane = lax.broadcasted_iota(jnp.int32, (tq, LANES), 1)
    head0 = lane < HEAD_DIM
    q = q_ref[...] * (HEAD_DIM ** -0.5)
    k = jnp.concatenate([k0_ref[...], k1_ref[...], k2_ref[...]], axis=0).astype(BF16)
    v = jnp.concatenate([v0_ref[...], v1_ref[...], v2_ref[...]], axis=0).astype(BF16)
    key_abs = lax.broadcasted_iota(jnp.int32, (tq, nk), 1) + (i * tq - left)
    valid = key_abs >= 0
    outs = []
    for h in range(2):
        qh = (jnp.where(head0, q, 0.0) if h == 0 else jnp.where(head0, 0.0, q)).astype(BF16)
        s = _dot_nt(qh, k) + bias_ref[h]
        s = jnp.where(valid, s, MASK_VALUE)
        m = jnp.max(s, axis=-1, keepdims=True)
        e = jnp.exp(s - m)
        l = jnp.sum(e, axis=-1, keepdims=True)
        outs.append(_dot(e.astype(BF16), v) / l)
    o_ref[...] = jnp.where(head0, outs[0], outs[1])


def _ca_attention(proj, rel_rows, B, S, tq):
    T = B * S
    nq = S // tq
    assert left_blocks(tq) == 2
    qcol = COL_CA // LANES
    kcol = (COL_CA + CA_WIDTH) // LANES
    vcol = (COL_CA + 2 * CA_WIDTH) // LANES

    def kv_spec(col, back):
        return pl.BlockSpec(
            (tq, LANES), lambda b, p, i: (b * nq + jnp.maximum(i - back, 0), col + p))

    return pl.pallas_call(
        functools.partial(_ca_kernel, tq=tq),
        grid=(B, CA_WIDTH // LANES, nq),
        in_specs=[
            pl.BlockSpec((tq, LANES), lambda b, p, i: (b * nq + i, qcol + p)),
            kv_spec(kcol, 2), kv_spec(kcol, 1), kv_spec(kcol, 0),
            kv_spec(vcol, 2), kv_spec(vcol, 1), kv_spec(vcol, 0),
            pl.BlockSpec((None, 2, 1, 4 * tq), lambda b, p, i: (p, 0, 0, 0)),
        ],
        out_specs=pl.BlockSpec((tq, LANES), lambda b, p, i: (b * nq + i, p)),
        out_shape=jax.ShapeDtypeStruct((T, CA_WIDTH), F32),
        scratch_shapes=[pltpu.VMEM((2, tq, 3 * tq), F32)],
        compiler_params=_cparams("parallel", "parallel", "arbitrary"),
        name="ca_attention",
    )(proj, proj, proj, proj, proj, proj, proj, rel_rows)


def left_blocks(tq):
    return (CA_LEFT_CHUNKS * CHUNK) // tq


def _seg_sum(x, bd):
    parts = [_split_dot(x[:, c:c + LANES], bd) for c in range(0, x.shape[1], LANES)]
    return jnp.concatenate(parts, axis=1)


def _rwprep_kernel(*refs, tm, tiles_per_seq, first_layer):
    if first_layer:
        (main_ref, lora_ref, pmain_ref, plora_ref, mum_ref, mul_ref, w0_ref, w2_ref, a0_ref,
         a2_ref, g2_ref, kkw_ref, kaw_ref, bd_ref,
         r_out, lw_out, k_out, v_out, kk_out, a_out, g_out, vf_out) = refs
    else:
        (main_ref, lora_ref, pmain_ref, plora_ref, vf_ref, mum_ref, mul_ref, w0_ref, w2_ref,
         a0_ref, a2_ref, v0_ref, v2_ref, g2_ref, kkw_ref, kaw_ref, bd_ref,
         r_out, lw_out, k_out, v_out, kk_out, a_out, g_out) = refs
    i = pl.program_id(0)
    seq_start = (i % tiles_per_seq) == 0

    def shift_lerp(x, prev8, mu):
        prev = jnp.where(seq_start, 0.0, prev8[7:8, :])
        row = lax.broadcasted_iota(jnp.int32, x.shape, 0)
        xs = jnp.where(row == 0, prev, pltpu.roll(x, 1, 0))
        return x + (xs - x) * mu

    cm = shift_lerp(main_ref[...], pmain_ref[...], mum_ref[...])
    cl = shift_lerp(lora_ref[...], plora_ref[...], mul_ref[...])
    r = cm[:, :RW_WIDTH]
    k = cm[:, RW_WIDTH:2 * RW_WIDTH]
    v = cm[:, 2 * RW_WIDTH:]
    wa = cl[:, :LORA_WA]
    gv = cl[:, LORA_WA:]

    log_w = -_softplus(-(w0_ref[...] + _dot(jnp.tanh(wa).astype(BF16), w2_ref[...]))) - 0.5
    lw_out[...] = -jnp.exp(log_w)
    a = _sigmoid(a0_ref[...] + _dot(wa.astype(BF16), a2_ref[...]))
    g_out[...] = _dot(_sigmoid(gv).astype(BF16), g2_ref[...])
    if first_layer:
        vf_out[...] = v
    else:
        mix = _sigmoid(v0_ref[...] + _dot(gv.astype(BF16), v2_ref[...]))
        v = v + (vf_ref[...] - v) * mix
    kk = k * kkw_ref[...]
    norm = jnp.sqrt(_seg_sum(kk * kk, bd_ref[...]))
    kk_out[...] = kk / jnp.maximum(norm, 1e-12)
    k_out[...] = k * (1.0 + (a - 1.0) * kaw_ref[...])
    r_out[...] = r
    v_out[...] = v
    a_out[...] = a


def _rwprep(proj, v_first, prm, B, S, tm, first_layer):
    T = B * S
    nt = T // tm
    sub = tm // 8
    main_blk = COL_RW // (3 * RW_WIDTH)
    lora_blk = COL_LORA // (LORA_WA + LORA_GV)
    assert main_blk * 3 * RW_WIDTH == COL_RW and lora_blk * (LORA_WA + LORA_GV) == COL_LORA
    wl = LORA_WA + LORA_GV

    def full(a):
        return pl.BlockSpec(a.shape, lambda i: (0,) * a.ndim)

    row = pl.BlockSpec((tm, RW_WIDTH), lambda i: (i, 0))
    in_specs = [
        pl.BlockSpec((tm, 3 * RW_WIDTH), lambda i: (i, main_blk)),
        pl.BlockSpec((tm, wl), lambda i: (i, lora_blk)),
        pl.BlockSpec((8, 3 * RW_WIDTH), lambda i: (jnp.maximum(i * sub - 1, 0), main_blk)),
        pl.BlockSpec((8, wl), lambda i: (jnp.maximum(i * sub - 1, 0), lora_blk)),
    ]
    args = [proj, proj, proj, proj]
    if not first_layer:
        in_specs.append(row)
        args.append(v_first)
    names = ["mu_main", "mu_lora", "w0", "w2", "a0", "a2"]
    if not first_layer:
        names += ["v0", "v2"]
    names += ["g2", "k_k", "k_a", "bd"]
    for n in names:
        in_specs.append(full(prm[n]))
        args.append(prm[n])
    n_out = 8 if first_layer else 7
    outs = pl.pallas_call(
        functools.partial(_rwprep_kernel, tm=tm, tiles_per_seq=S // tm, first_layer=first_layer),
        grid=(nt,),
        in_specs=in_specs,
        out_specs=[row] * n_out,
        out_shape=[jax.ShapeDtypeStruct((T, RW_WIDTH), F32)] * n_out,
        compiler_params=_cparams("parallel"),
        name="rwkv_prep",
    )(*args)
    return outs


def _wkv_kernel(r_ref, lw_ref, k_ref, v_ref, kk_ref, a_ref, g_ref, rk_ref, lnw_ref, lnb_ref,
                tri_ref, bd_ref, o_ref, state_ref, y_ref, *, tb):
    C = CHUNK
    n_pairs = RW_WIDTH // LANES

    @pl.when(pl.program_id(1) == 0)
    def _():
        state_ref[...] = jnp.zeros_like(state_ref)

    lane = lax.broadcasted_iota(jnp.int32, (C, LANES), 1)
    head0 = lane < HEAD_DIM
    rows2 = lax.broadcasted_iota(jnp.int32, (2 * C, 2 * C), 0)
    cols2 = lax.broadcasted_iota(jnp.int32, (2 * C, 2 * C), 1)
    same = (rows2 // C) == (cols2 // C)
    strict = same & (cols2 < rows2)
    incl = same & (cols2 <= rows2)
    eye = (rows2 == cols2).astype(F32)
    blk_xor = rows2 ^ cols2
    tri = tri_ref[...]

    def stack(x):
        return jnp.concatenate([jnp.where(head0, x, 0.0), jnp.where(head0, 0.0, x)], axis=0)

    def chunk(c, carry):
        rows = pl.ds(pl.multiple_of(c * C, C), C)
        for p in range(n_pairs):
            cols = slice(p * LANES, (p + 1) * LANES)
            r, lw, k, v = r_ref[rows, cols], lw_ref[rows, cols], k_ref[rows, cols], v_ref[rows, cols]
            kk, a = kk_ref[rows, cols], a_ref[rows, cols]
            cl = _split_dot_left(tri, lw)
            e_pos = jnp.exp(cl)
            e_neg = jnp.exp(-cl)
            rt = stack(r * e_pos).astype(BF16)
            at = stack(-kk * jnp.exp(cl - lw)).astype(BF16)
            bt = stack(kk * a * e_neg).astype(BF16)
            kt = stack(k * e_neg).astype(BF16)
            vs = stack(v).astype(BF16)
            g_state = state_ref[p]
            gb = g_state.astype(BF16)

            ar = jnp.concatenate([at, rt], axis=0)
            bk = jnp.concatenate([bt, kt], axis=0)
            prod = _dot_nt(ar, bk)
            l_ab = jnp.where(strict, prod[:2 * C, :2 * C], 0.0)
            l_ak = jnp.where(strict, prod[:2 * C, 2 * C:], 0.0)
            q_rb = jnp.where(incl, prod[2 * C:, :2 * C], 0.0)
            q_rk = jnp.where(incl, prod[2 * C:, 2 * C:], 0.0)

            pm = eye + jnp.where(blk_xor == 1, l_ab, 0.0)
            s = 2
            while s < C:
                e = jnp.where((blk_xor >= s) & (blk_xor < 2 * s), l_ab, 0.0).astype(BF16)
                pb = pm.astype(BF16)
                pm = pm + _dot(pb, _dot(e, pb).astype(BF16))
                s *= 2
            from_state = _dot_nt(ar, gb)
            x = from_state[:2 * C] + _dot(l_ak.astype(BF16), vs)
            u = _dot(pm.astype(BF16), x.astype(BF16))
            ub = u.astype(BF16)
            ys = from_state[2 * C:] + _dot(q_rb.astype(BF16), ub) + _dot(q_rk.astype(BF16), vs)
            y_ref[rows, cols] = ys[:C] + ys[C:]
            upd = _dot_tn(jnp.concatenate([ub, vs], axis=0), bk)
            state_ref[p] = (g_state + upd) * e_pos[C - 1:C, :]
        return carry

    lax.fori_loop(0, tb // C, chunk, 0)

    bd = bd_ref[...]
    y = y_ref[...]
    mean = _seg_sum(y, bd) * (1.0 / HEAD_DIM)
    d = y - mean
    var = _seg_sum(d * d, bd) * (1.0 / HEAD_DIM)
    yn = d * lax.rsqrt(var + GN_EPS) * lnw_ref[...] + lnb_ref[...]
    bonus = _seg_sum(r_ref[...] * k_ref[...] * rk_ref[...], bd) * v_ref[...]
    o_ref[...] = (yn + bonus) * g_ref[...]


def _wkv(feats, prm, B, S, tb):
    T = B * S
    nt = S // tb
    r, lw, k, v, kk, a, g = feats
    row = pl.BlockSpec((tb, RW_WIDTH), lambda b, t: (b * nt + t, 0))

    def full(x):
        return pl.BlockSpec(x.shape, lambda b, t: (0,) * x.ndim)

    small = [prm["r_k"], prm["ln_w"], prm["ln_b"], prm["tri_chunk"], prm["bd"]]
    return pl.pallas_call(
        functools.partial(_wkv_kernel, tb=tb),
        grid=(B, nt),
        in_specs=[row] * 7 + [full(x) for x in small],
        out_specs=row,
        out_shape=jax.ShapeDtypeStruct((T, RW_WIDTH), F32),
        scratch_shapes=[pltpu.VMEM((RW_WIDTH // LANES, LANES, LANES), F32),
                        pltpu.VMEM((tb, RW_WIDTH), F32)],
        compiler_params=_cparams("parallel", "arbitrary"),
        name="wkv7",
    )(r, lw, k, v, kk, a, g, *small)


def _outffn_kernel(x_ref, sb_ref, ca_ref, rw_ref, sbg_ref, cag_ref, wo_ref, fg_ref, w1_ref, w2_ref,
                   fin_ref, o_ref, h_ref, acc_ref, *, final_norm):
    j = pl.program_id(1)

    @pl.when(j == 0)
    def _():
        sbn = _rms(sb_ref[...], sbg_ref[...]).astype(BF16)
        can = _rms(ca_ref[...], cag_ref[...]).astype(BF16)
        x = x_ref[...]
        x = x + _dot(sbn, wo_ref[:SB_WIDTH, :])
        x = x + _dot(can, wo_ref[SB_WIDTH:SB_WIDTH + CA_WIDTH, :])
        x = x + _dot(rw_ref[...].astype(BF16), wo_ref[SB_WIDTH + CA_WIDTH:, :])
        acc_ref[...] = x
        h_ref[...] = _rms(x, fg_ref[...]).astype(BF16)

    f = jnp.square(jnp.maximum(_dot(h_ref[...], w1_ref[...]), 0.0)).astype(BF16)
    acc_ref[...] += _dot(f, w2_ref[...])

    @pl.when(j == pl.num_programs(1) - 1)
    def _():
        y = acc_ref[...]
        if final_norm:
            y = _rms(y, fin_ref[...])
        o_ref[...] = y


def _outffn(x2d, sb, ca, rw, prm, tm, tf, final_norm):
    T, D = x2d.shape
    Fd = prm["w_ff_in"].shape[1]

    def rows(w):
        return pl.BlockSpec((tm, w), lambda i, j: (i, 0))

    def full(a):
        return pl.BlockSpec(a.shape, lambda i, j: (0,) * a.ndim)

    return pl.pallas_call(
        functools.partial(_outffn_kernel, final_norm=final_norm),
        grid=(T // tm, Fd // tf),
        in_specs=[
            rows(D), rows(SB_WIDTH), rows(CA_WIDTH), rows(RW_WIDTH),
            full(prm["sb_g"]), full(prm["ca_g"]), full(prm["w_out"]), full(prm["ffn_g"]),
            pl.BlockSpec((D, tf), lambda i, j: (0, j)),
            pl.BlockSpec((tf, D), lambda i, j: (j, 0)),
            full(prm["final_g"]),
        ],
        out_specs=rows(D),
        out_shape=jax.ShapeDtypeStruct((T, D), F32),
        scratch_shapes=[pltpu.VMEM((tm, D), BF16), pltpu.VMEM((tm, D), F32)],
        compiler_params=_cparams("parallel", "arbitrary"),
        name="outproj_ffn",
    )(x2d, sb, ca, rw, prm["sb_g"], prm["ca_g"], prm["w_out"], prm["ffn_g"],
      prm["w_ff_in"], prm["w_ff_out"], prm["final_g"])


def _layer_params(l, p, ca_tq):
    f32 = F32
    w_in = p["w_in"][l]
    D = w_in.shape[0]
    attn = w_in[:, :COL_RW]
    rest = w_in[:, COL_RW:]
    o_w = RW_WIDTH
    o_k = o_w + D_DECAY_LORA
    o_v = o_k + RW_WIDTH
    o_a = o_v + RW_WIDTH
    o_g = o_a + D_AAA_LORA
    r_c, w_c, k_c, v_c = rest[:, :o_w], rest[:, o_w:o_k], rest[:, o_k:o_v], rest[:, o_v:o_a]
    a_c, g_c = rest[:, o_a:o_g], rest[:, o_g:]
    pad_gv = LORA_GV - D_GATE_LORA - D_MV_LORA
    if l == 0:
        vd_c = jnp.zeros((D, D_MV_LORA), f32)
    else:
        vd_c = p["w_vmix_down"][l - 1]
    w_proj = jnp.concatenate(
        [attn, r_c, k_c, v_c, w_c, a_c, g_c, vd_c, jnp.zeros((D, pad_gv), f32)], axis=1)
    mu = p["rw_mu"][l]
    mu_main = jnp.concatenate([mu[:o_w], mu[o_k:o_v], mu[o_v:o_a]])[None, :]
    mu_lora = jnp.concatenate(
        [mu[o_w:o_k], mu[o_a:o_g], mu[o_g:], jnp.zeros((D_MV_LORA + pad_gv,), f32)])[None, :]

    def padrows(w, before, total):
        return jnp.pad(w, ((before, total - before - w.shape[0]), (0, 0))).astype(BF16)

    prm = {
        "norm_g": p["norm_mix_g"][l][None, :],
        "w_proj": w_proj.astype(BF16),
        "mu_main": mu_main,
        "mu_lora": mu_lora,
        "w0": p["rw_w0"][l][None, :],
        "w2": padrows(p["rw_w2"][l], 0, LORA_WA),
        "a0": p["rw_a0"][l][None, :],
        "a2": padrows(p["rw_a2"][l], D_DECAY_LORA, LORA_WA),
        "g2": padrows(p["rw_g2"][l], 0, LORA_GV),
        "k_k": p["rw_k_k"][l][None, :],
        "k_a": p["rw_k_a"][l][None, :],
        "r_k": p["rw_r_k"][l][None, :],
        "ln_w": p["rw_ln_w"][l][None, :],
        "ln_b": p["rw_ln_b"][l][None, :],
        "sb_g": p["sb_out_g"][l][None, :],
        "ca_g": p["ca_out_g"][l][None, :],
        "w_out": p["w_out"][l].astype(BF16),
        "ffn_g": p["norm_ffn_g"][l][None, :],
        "w_ff_in": p["w_ff_in"][l].astype(BF16),
        "w_ff_out": p["w_ff_out"][l].astype(BF16),
        "final_g": p["norm_final_g"][None, :],
    }
    if l > 0:
        prm["v0"] = p["rw_v0"][l - 1][None, :]
        prm["v2"] = padrows(p["rw_v2"][l - 1], D_GATE_LORA, LORA_GV)
    table = p["ca_rel_bias"][l]
    left = CA_LEFT_CHUNKS * CHUNK + ca_tq - REL_CLIP
    right = 4 * ca_tq - left - table.shape[1]
    rel = jnp.pad(table, ((0, 0), (left, right)), mode="edge")
    prm["rel_rows"] = rel.reshape(CA_WIDTH // LANES, 2, 1, 4 * ca_tq)
    return prm


def _constants(sb_tq):
    idx = jnp.arange(sb_tq)
    tri_sb = (idx[:, None] > idx[None, :]).astype(BF16)
    c = jnp.arange(CHUNK)
    tri_chunk = (c[:, None] >= c[None, :]).astype(BF16)
    ln = jnp.arange(LANES)
    bd = ((ln[:, None] // HEAD_DIM) == (ln[None, :] // HEAD_DIM)).astype(BF16)
    return tri_sb, tri_chunk, bd


def _forward(x, p, *, tm_proj, sb_tq, ca_tq, tm_prep, tb_wkv, tm_ffn, tf_ffn):
    B, S, D = x.shape
    depth = p["w_in"].shape[0]
    tri_sb, tri_chunk, bd = _constants(sb_tq)
    x2d = x.reshape(B * S, D)
    v_first = None
    for l in range(depth):
        prm = _layer_params(l, p, ca_tq)
        prm["tri_chunk"] = tri_chunk
        prm["bd"] = bd
        proj = _inproj(x2d, prm["norm_g"], prm["w_proj"], tm_proj)
        sb = _sb_attention(proj, tri_sb, B, S, sb_tq)
        ca = _ca_attention(proj, prm["rel_rows"], B, S, ca_tq)
        feats = _rwprep(proj, v_first, prm, B, S, tm_prep, l == 0)
        if l == 0:
            v_first = feats[7]
        rw = _wkv(feats[:7], prm, B, S, tb_wkv)
        x2d = _outffn(x2d, sb, ca, rw, prm, tm_ffn, tf_ffn, l == depth - 1)
    return x2d.reshape(B, S, D)


def kernel(x, norm_mix_g, w_in, w_vmix_down, sb_out_g, ca_rel_bias, ca_out_g, rw_mu, rw_w0, rw_w2,
           rw_a0, rw_a2, rw_v0, rw_v2, rw_g2, rw_k_k, rw_k_a, rw_r_k, rw_ln_w, rw_ln_b, w_out,
           norm_ffn_g, w_ff_in, w_ff_out, norm_final_g):
    p = dict(norm_mix_g=norm_mix_g, w_in=w_in, w_vmix_down=w_vmix_down, sb_out_g=sb_out_g,
             ca_rel_bias=ca_rel_bias, ca_out_g=ca_out_g, rw_mu=rw_mu, rw_w0=rw_w0, rw_w2=rw_w2,
             rw_a0=rw_a0, rw_a2=rw_a2, rw_v0=rw_v0, rw_v2=rw_v2, rw_g2=rw_g2, rw_k_k=rw_k_k,
             rw_k_a=rw_k_a, rw_r_k=rw_r_k, rw_ln_w=rw_ln_w, rw_ln_b=rw_ln_b, w_out=w_out,
             norm_ffn_g=norm_ffn_g, w_ff_in=w_ff_in, w_ff_out=w_ff_out, norm_final_g=norm_final_g)
    return _forward(x, p, tm_proj=512, sb_tq=256, ca_tq=256, tm_prep=512, tb_wkv=512,
                    tm_ffn=512, tf_ffn=512)
```

```python
import functools

import jax
import jax.numpy as jnp
from jax import lax
from jax.experimental import pallas as pl
from jax.experimental.pallas import tpu as pltpu

F32 = jnp.float32
BF16 = jnp.bfloat16

LANES = 128
HEAD_DIM = 64
CHUNK = 64
SB_WIDTH = 256
CA_WIDTH = 256
RW_WIDTH = 512
CA_LEFT_CHUNKS = 8
REL_CLIP = 256
D_DECAY_LORA = 64
D_AAA_LORA = 64
D_GATE_LORA = 160
D_MV_LORA = 32
RMS_EPS = 1e-5
GN_EPS = 64e-5
MASK_VALUE = -1e30
SB_DEAD_LOG = -104.0

COL_SB = 0
COL_CA = 3 * SB_WIDTH
COL_RW = COL_CA + 3 * CA_WIDTH
LORA_WA = 2 * HEAD_DIM
LORA_GV = 256
COL_LORA = COL_RW + 3 * RW_WIDTH
N_PROJ = COL_LORA + LORA_WA + LORA_GV

VMEM_LIMIT = 56 * 1024 * 1024


def _cparams(*sem):
    return pltpu.CompilerParams(dimension_semantics=sem, vmem_limit_bytes=VMEM_LIMIT)


def _dot(a, b):
    return jnp.dot(a, b, preferred_element_type=F32)


def _dot_nt(a, b):
    return lax.dot_general(a, b, (((1,), (1,)), ((), ())), preferred_element_type=F32)


def _dot_tn(a, b):
    return lax.dot_general(a, b, (((0,), (0,)), ((), ())), preferred_element_type=F32)


def _split_dot(x, m):
    hi = x.astype(BF16)
    lo = (x - hi.astype(F32)).astype(BF16)
    return _dot(hi, m) + _dot(lo, m)


def _split_dot_left(m, x):
    hi = x.astype(BF16)
    lo = (x - hi.astype(F32)).astype(BF16)
    return _dot(m, hi) + _dot(m, lo)


def _softplus(x):
    return jnp.maximum(x, 0.0) + jnp.log1p(jnp.exp(-jnp.abs(x)))


def _sigmoid(x):
    return 1.0 / (1.0 + jnp.exp(-x))


def _rms(x, g):
    return x * lax.rsqrt(jnp.mean(x * x, axis=-1, keepdims=True) + RMS_EPS) * g


def _inproj_kernel(x_ref, g_ref, w_ref, o_ref):
    h = _rms(x_ref[...], g_ref[...]).astype(BF16)
    o_ref[...] = _dot(h, w_ref[...])


def _inproj(x2d, g, w, tm):
    T, D = x2d.shape
    N = w.shape[1]
    return pl.pallas_call(
        _inproj_kernel,
        grid=(T // tm,),
        in_specs=[
            pl.BlockSpec((tm, D), lambda i: (i, 0)),
            pl.BlockSpec((1, D), lambda i: (0, 0)),
            pl.BlockSpec((D, N), lambda i: (0, 0)),
        ],
        out_specs=pl.BlockSpec((tm, N), lambda i: (i, 0)),
        out_shape=jax.ShapeDtypeStruct((T, N), F32),
        compiler_params=_cparams("parallel"),
        name="inproj",
    )(x2d, g, w)


def _sb_kernel(q_ref, k_ref, v_ref, tri_ref, o_ref, acc_ref, car_ref, *, tq):
    i = pl.program_id(2)
    lane = lax.broadcasted_iota(jnp.int32, (tq, LANES), 1)
    head0 = lane < HEAD_DIM
    q = q_ref[...] * (HEAD_DIM ** -0.5)
    qh = (jnp.where(head0, q, 0.0).astype(BF16), jnp.where(head0, 0.0, q).astype(BF16))
    tri = tri_ref[...]
    strict = (lax.broadcasted_iota(jnp.int32, (tq, tq), 1)
              < lax.broadcasted_iota(jnp.int32, (tq, tq), 0))
    acc_ref[...] = jnp.zeros_like(acc_ref)
    car_ref[...] = jnp.zeros_like(car_ref)

    def tile(kb, diag):
        start = pl.multiple_of(kb * tq, tq)
        kblk = k_ref[pl.ds(start, tq), :].astype(BF16)
        vblk = v_ref[pl.ds(start, tq), :].astype(BF16)
        for h in range(2):
            z = _dot_nt(qh[h], kblk)
            sp = _softplus(z)
            log_beta = z - sp
            log_1m = -sp
            if diag:
                log_1m = jnp.where(strict, log_1m, 0.0)
            after = _split_dot(log_1m, tri)
            carry = car_ref[h]
            w = jnp.exp(log_beta + after + carry)
            if diag:
                w = jnp.where(strict, w, 0.0)
            acc_ref[h] += _dot(w.astype(BF16), vblk)
            car_ref[h] = carry + after[:, :1] + log_1m[:, :1]

    tile(i, True)

    def live():
        return jnp.max(car_ref[...]) > SB_DEAD_LOG

    def cond(s):
        return (s[0] < i) & s[1]

    def body(s):
        tile(i - 1 - s[0], False)
        return s[0] + 1, live()

    lax.while_loop(cond, body, (jnp.int32(0), live()))
    o_ref[...] = jnp.where(head0, acc_ref[0], acc_ref[1])


def _sb_attention(proj, tri, B, S, tq):
    T = B * S
    nq = S // tq
    kern = functools.partial(_sb_kernel, tq=tq)
    qcol = COL_SB // LANES
    kcol = (COL_SB + SB_WIDTH) // LANES
    vcol = (COL_SB + 2 * SB_WIDTH) // LANES
    return pl.pallas_call(
        kern,
        grid=(B, SB_WIDTH // LANES, nq),
        in_specs=[
            pl.BlockSpec((tq, LANES), lambda b, p, i: (b * nq + i, qcol + p)),
            pl.BlockSpec((S, LANES), lambda b, p, i: (b, kcol + p)),
            pl.BlockSpec((S, LANES), lambda b, p, i: (b, vcol + p)),
            pl.BlockSpec((tq, tq), lambda b, p, i: (0, 0)),
        ],
        out_specs=pl.BlockSpec((tq, LANES), lambda b, p, i: (b * nq + i, p)),
        out_shape=jax.ShapeDtypeStruct((T, SB_WIDTH), F32),
        scratch_shapes=[pltpu.VMEM((2, tq, LANES), F32), pltpu.VMEM((2, tq, 1), F32)],
        compiler_params=_cparams("parallel", "parallel", "arbitrary"),
        name="sb_attention",
    )(proj, proj, proj, tri)


def _ca_kernel(q_ref, k0_ref, k1_ref, k2_ref, v0_ref, v1_ref, v2_ref, rel_ref, o_ref, bias_ref,
               *, tq):
    i = pl.program_id(2)
    nk = 3 * tq
    left = CA_LEFT_CHUNKS * CHUNK

    @pl.when(i == 0)
    def _():
        qi = lax.broadcasted_iota(jnp.int32, (tq, nk), 0)
        kj = lax.broadcasted_iota(jnp.int32, (tq, nk), 1)
        qc = qi // CHUNK
        kc = kj // CHUNK
        band = (kc >= qc) & (kc <= qc + CA_LEFT_CHUNKS)
        for h in range(2):
            row = jnp.broadcast_to(rel_ref[h], (tq, rel_ref.shape[-1]))
            toep = pltpu.roll(row, rel_ref.shape[-1] - tq, 1, stride=1, stride_axis=0)
            bias_ref[h] = jnp.where(band, toep[:, :nk], MASK_VALUE)

    lane = lax.broadcasted_iota(jnp.int32, (tq, LANES), 1)
    head0 = lane < HEAD_DIM
    q = q_ref[...] * (HEAD_DIM ** -0.5)
    k = jnp.concatenate([k0_ref[...], k1_ref[...], k2_ref[...]], axis=0).astype(BF16)
    v = jnp.concatenate([v0_ref[...], v1_ref[...], v2_ref[...]], axis=0).astype(BF16)
    key_abs = lax.broadcasted_iota(jnp.int32, (tq, nk), 1) + (i * tq - left)
    valid = key_abs >= 0
    outs = []
    for h in range(2):
        qh = (jnp.where(head0, q, 0.0) if h == 0 else jnp.where(head0, 0.0, q)).astype(BF16)
        s = _dot_nt(qh, k) + bias_ref[h]
        s = jnp.where(valid, s, MASK_VALUE)
        m = jnp.max(s, axis=-1, keepdims=True)
        e = jnp.exp(s - m)
        l = jnp.sum(e, axis=-1, keepdims=True)
        outs.append(_dot(e.astype(BF16), v) / l)
    o_ref[...] = jnp.where(head0, outs[0], outs[1])


def _ca_attention(proj, rel_rows, B, S, tq):
    T = B * S
    nq = S // tq
    assert left_blocks(tq) == 2
    qcol = COL_CA // LANES
    kcol = (COL_CA + CA_WIDTH) // LANES
    vcol = (COL_CA + 2 * CA_WIDTH) // LANES

    def kv_spec(col, back):
        return pl.BlockSpec(
            (tq, LANES), lambda b, p, i: (b * nq + jnp.maximum(i - back, 0), col + p))

    return pl.pallas_call(
        functools.partial(_ca_kernel, tq=tq),
        grid=(B, CA_WIDTH // LANES, nq),
        in_specs=[
            pl.BlockSpec((tq, LANES), lambda b, p, i: (b * nq + i, qcol + p)),
            kv_spec(kcol, 2), kv_spec(kcol, 1), kv_spec(kcol, 0),
            kv_spec(vcol, 2), kv_spec(vcol, 1), kv_spec(vcol, 0),
            pl.BlockSpec((None, 2, 1, 4 * tq), lambda b, p, i: (p, 0, 0, 0)),
        ],
        out_specs=pl.BlockSpec((tq, LANES), lambda b, p, i: (b * nq + i, p)),
        out_shape=jax.ShapeDtypeStruct((T, CA_WIDTH), F32),
        scratch_shapes=[pltpu.VMEM((2, tq, 3 * tq), F32)],
        compiler_params=_cparams("parallel", "parallel", "arbitrary"),
        name="ca_attention",
    )(proj, proj, proj, proj, proj, proj, proj, rel_rows)


def left_blocks(tq):
    return (CA_LEFT_CHUNKS * CHUNK) // tq


def _seg_sum(x, bd):
    parts = [_split_dot(x[:, c:c + LANES], bd) for c in range(0, x.shape[1], LANES)]
    return jnp.concatenate(parts, axis=1)


def _rwprep_kernel(*refs, tm, tiles_per_seq, first_layer):
    if first_layer:
        (main_ref, lora_ref, pmain_ref, plora_ref, mum_ref, mul_ref, w0_ref, w2_ref, a0_ref,
         a2_ref, g2_ref, kkw_ref, kaw_ref, bd_ref,
         r_out, lw_out, k_out, v_out, kk_out, a_out, g_out, vf_out) = refs
    else:
        (main_ref, lora_ref, pmain_ref, plora_ref, vf_ref, mum_ref, mul_ref, w0_ref, w2_ref,
         a0_ref, a2_ref, v0_ref, v2_ref, g2_ref, kkw_ref, kaw_ref, bd_ref,
         r_out, lw_out, k_out, v_out, kk_out, a_out, g_out) = refs
    i = pl.program_id(0)
    seq_start = (i % tiles_per_seq) == 0

    def shift_lerp(x, prev8, mu):
        prev = jnp.where(seq_start, 0.0, prev8[7:8, :])
        row = lax.broadcasted_iota(jnp.int32, x.shape, 0)
        xs = jnp.where(row == 0, prev, pltpu.roll(x, 1, 0))
        return x + (xs - x) * mu

    cm = shift_lerp(main_ref[...], pmain_ref[...], mum_ref[...])
    cl = shift_lerp(lora_ref[...], plora_ref[...], mul_ref[...])
    r = cm[:, :RW_WIDTH]
    k = cm[:, RW_WIDTH:2 * RW_WIDTH]
    v = cm[:, 2 * RW_WIDTH:]
    wa = cl[:, :LORA_WA]
    gv = cl[:, LORA_WA:]

    log_w = -_softplus(-(w0_ref[...] + _dot(jnp.tanh(wa).astype(BF16), w2_ref[...]))) - 0.5
    lw_out[...] = -jnp.exp(log_w)
    a = _sigmoid(a0_ref[...] + _dot(wa.astype(BF16), a2_ref[...]))
    g_out[...] = _dot(_sigmoid(gv).astype(BF16), g2_ref[...])
    if first_layer:
        vf_out[...] = v
    else:
        mix = _sigmoid(v0_ref[...] + _dot(gv.astype(BF16), v2_ref[...]))
        v = v + (vf_ref[...] - v) * mix
    kk = k * kkw_ref[...]
    norm = jnp.sqrt(_seg_sum(kk * kk, bd_ref[...]))
    kk_out[...] = kk / jnp.maximum(norm, 1e-12)
    k_out[...] = k * (1.0 + (a - 1.0) * kaw_ref[...])
    r_out[...] = r
    v_out[...] = v
    a_out[...] = a


def _rwprep(proj, v_first, prm, B, S, tm, first_layer):
    T = B * S
    nt = T // tm
    sub = tm // 8
    main_blk = COL_RW // (3 * RW_WIDTH)
    lora_blk = COL_LORA // (LORA_WA + LORA_GV)
    assert main_blk * 3 * RW_WIDTH == COL_RW and lora_blk * (LORA_WA + LORA_GV) == COL_LORA
    wl = LORA_WA + LORA_GV

    def full(a):
        return pl.BlockSpec(a.shape, lambda i: (0,) * a.ndim)

    row = pl.BlockSpec((tm, RW_WIDTH), lambda i: (i, 0))
    in_specs = [
        pl.BlockSpec((tm, 3 * RW_WIDTH), lambda i: (i, main_blk)),
        pl.BlockSpec((tm, wl), lambda i: (i, lora_blk)),
        pl.BlockSpec((8, 3 * RW_WIDTH), lambda i: (jnp.maximum(i * sub - 1, 0), main_blk)),
        pl.BlockSpec((8, wl), lambda i: (jnp.maximum(i * sub - 1, 0), lora_blk)),
    ]
    args = [proj, proj, proj, proj]
    if not first_layer:
        in_specs.append(row)
        args.append(v_first)
    names = ["mu_main", "mu_lora", "w0", "w2", "a0", "a2"]
    if not first_layer:
        names += ["v0", "v2"]
    names += ["g2", "k_k", "k_a", "bd"]
    for n in names:
        in_specs.append(full(prm[n]))
        args.append(prm[n])
    n_out = 8 if first_layer else 7
    outs = pl.pallas_call(
        functools.partial(_rwprep_kernel, tm=tm, tiles_per_seq=S // tm, first_layer=first_layer),
        grid=(nt,),
        in_specs=in_specs,
        out_specs=[row] * n_out,
        out_shape=[jax.ShapeDtypeStruct((T, RW_WIDTH), F32)] * n_out,
        compiler_params=_cparams("parallel"),
        name="rwkv_prep",
    )(*args)
    return outs


def _wkv_kernel(r_ref, lw_ref, k_ref, v_ref, kk_ref, a_ref, g_ref, rk_ref, lnw_ref, lnb_ref,
                tri_ref, bd_ref, o_ref, state_ref, y_ref, *, tb):
    C = CHUNK
    n_pairs = RW_WIDTH // LANES

    @pl.when(pl.program_id(1) == 0)
    def _():
        state_ref[...] = jnp.zeros_like(state_ref)

    lane = lax.broadcasted_iota(jnp.int32, (C, LANES), 1)
    head0 = lane < HEAD_DIM
    rows2 = lax.broadcasted_iota(jnp.int32, (2 * C, 2 * C), 0)
    cols2 = lax.broadcasted_iota(jnp.int32, (2 * C, 2 * C), 1)
    same = (rows2 // C) == (cols2 // C)
    strict = same & (cols2 < rows2)
    incl = same & (cols2 <= rows2)
    eye = (rows2 == cols2).astype(F32)
    blk_xor = rows2 ^ cols2
    tri = tri_ref[...]

    def stack(x):
        return jnp.concatenate([jnp.where(head0, x, 0.0), jnp.where(head0, 0.0, x)], axis=0)

    def prepare(c):
        rows = pl.ds(pl.multiple_of(c * C, C), C)
        P = range(n_pairs)

        def load(ref):
            return [ref[rows, p * LANES:(p + 1) * LANES] for p in P]

        r, lw, k, v, kk, a = (load(x) for x in (r_ref, lw_ref, k_ref, v_ref, kk_ref, a_ref))
        cl = [_split_dot_left(tri, lw[p]) for p in P]
        e_pos = [jnp.exp(cl[p]) for p in P]
        e_neg = [jnp.exp(-cl[p]) for p in P]
        rt = [stack(r[p] * e_pos[p]).astype(BF16) for p in P]
        at = [stack(-kk[p] * jnp.exp(cl[p] - lw[p])).astype(BF16) for p in P]
        bt = [stack(kk[p] * a[p] * e_neg[p]).astype(BF16) for p in P]
        kt = [stack(k[p] * e_neg[p]).astype(BF16) for p in P]
        vs = [stack(v[p]).astype(BF16) for p in P]
        prod = [_dot_nt(jnp.concatenate([at[p], rt[p]], axis=0),
                        jnp.concatenate([bt[p], kt[p]], axis=0)) for p in P]
        l_ab = [jnp.where(strict, prod[p][:2 * C, :2 * C], 0.0) for p in P]
        l_ak = [jnp.where(strict, prod[p][:2 * C, 2 * C:], 0.0).astype(BF16) for p in P]
        q_rb = [jnp.where(incl, prod[p][2 * C:, :2 * C], 0.0).astype(BF16) for p in P]
        q_rk = [jnp.where(incl, prod[p][2 * C:, 2 * C:], 0.0).astype(BF16) for p in P]
        lakv = [_dot(l_ak[p], vs[p]).astype(BF16) for p in P]
        y0 = [_dot(q_rk[p], vs[p]) for p in P]
        kv = [_dot_tn(vs[p], kt[p]) for p in P]

        pm = [eye + jnp.where(blk_xor == 1, l_ab[p], 0.0) for p in P]
        s = 2
        while s < C:
            level = (blk_xor >= s) & (blk_xor < 2 * s)
            e = [jnp.where(level, l_ab[p], 0.0).astype(BF16) for p in P]
            pb = [pm[p].astype(BF16) for p in P]
            et = [_dot(e[p], pb[p]).astype(BF16) for p in P]
            pm = [pm[p] + _dot(pb[p], et[p]) for p in P]
            s *= 2
        tw = [_dot(pm[p].astype(BF16), jnp.concatenate([at[p], lakv[p]], axis=1)) for p in P]
        return tuple(
            (jnp.concatenate([tw[p][:, :LANES].astype(BF16), rt[p]], axis=0), tw[p][:, LANES:],
             q_rb[p], y0[p][:C] + y0[p][C:], bt[p], kv[p], e_pos[p][C - 1:C, :]) for p in P)

    def advance(c, prepared):
        rows = pl.ds(pl.multiple_of(c * C, C), C)
        states = [state_ref[p] for p in range(n_pairs)]
        P = range(n_pairs)
        w_r, u0, q_rb, y0, bt, kv, decay = zip(*prepared)
        from_state = [_dot_nt(w_r[p], states[p].astype(BF16)) for p in P]
        ub = [(from_state[p][:2 * C] + u0[p]).astype(BF16) for p in P]
        new_states = [(states[p] + _dot_tn(ub[p], bt[p]) + kv[p]) * decay[p] for p in P]
        y2 = [from_state[p][2 * C:] + _dot(q_rb[p], ub[p]) for p in P]
        ys = [y0[p] + y2[p][:C] + y2[p][C:] for p in P]
        for p in P:
            state_ref[p] = new_states[p]
            y_ref[rows, p * LANES:(p + 1) * LANES] = ys[p]

    n_chunks = tb // C

    def body(c, prepared):
        nxt = prepare(c)
        advance(c - 1, prepared)
        return nxt

    last = lax.fori_loop(1, n_chunks, body, prepare(0))
    advance(n_chunks - 1, last)

    bd = bd_ref[...]
    y = y_ref[...]
    mean = _seg_sum(y, bd) * (1.0 / HEAD_DIM)
    d = y - mean
    var = _seg_sum(d * d, bd) * (1.0 / HEAD_DIM)
    yn = d * lax.rsqrt(var + GN_EPS) * lnw_ref[...] + lnb_ref[...]
    bonus = _seg_sum(r_ref[...] * k_ref[...] * rk_ref[...], bd) * v_ref[...]
    o_ref[...] = (yn + bonus) * g_ref[...]


def _wkv(feats, prm, B, S, tb):
    T = B * S
    nt = S // tb
    r, lw, k, v, kk, a, g = feats
    row = pl.BlockSpec((tb, RW_WIDTH), lambda b, t: (b * nt + t, 0))

    def full(x):
        return pl.BlockSpec(x.shape, lambda b, t: (0,) * x.ndim)

    small = [prm["r_k"], prm["ln_w"], prm["ln_b"], prm["tri_chunk"], prm["bd"]]
    return pl.pallas_call(
        functools.partial(_wkv_kernel, tb=tb),
        grid=(B, nt),
        in_specs=[row] * 7 + [full(x) for x in small],
        out_specs=row,
        out_shape=jax.ShapeDtypeStruct((T, RW_WIDTH), F32),
        scratch_shapes=[pltpu.VMEM((RW_WIDTH // LANES, LANES, LANES), F32),
                        pltpu.VMEM((tb, RW_WIDTH), F32)],
        compiler_params=_cparams("parallel", "arbitrary"),
        name="wkv7",
    )(r, lw, k, v, kk, a, g, *small)


def _outffn_kernel(x_ref, sb_ref, ca_ref, rw_ref, sbg_ref, cag_ref, wo_ref, fg_ref, w1_ref, w2_ref,
                   fin_ref, o_ref, h_ref, acc_ref, *, final_norm):
    j = pl.program_id(1)

    @pl.when(j == 0)
    def _():
        sbn = _rms(sb_ref[...], sbg_ref[...]).astype(BF16)
        can = _rms(ca_ref[...], cag_ref[...]).astype(BF16)
        x = x_ref[...]
        x = x + _dot(sbn, wo_ref[:SB_WIDTH, :])
        x = x + _dot(can, wo_ref[SB_WIDTH:SB_WIDTH + CA_WIDTH, :])
        x = x + _dot(rw_ref[...].astype(BF16), wo_ref[SB_WIDTH + CA_WIDTH:, :])
        acc_ref[...] = x
        h_ref[...] = _rms(x, fg_ref[...]).astype(BF16)

    f = jnp.square(jnp.maximum(_dot(h_ref[...], w1_ref[...]), 0.0)).astype(BF16)
    acc_ref[...] += _dot(f, w2_ref[...])

    @pl.when(j == pl.num_programs(1) - 1)
    def _():
        y = acc_ref[...]
        if final_norm:
            y = _rms(y, fin_ref[...])
        o_ref[...] = y


def _outffn(x2d, sb, ca, rw, prm, tm, tf, final_norm):
    T, D = x2d.shape
    Fd = prm["w_ff_in"].shape[1]

    def rows(w):
        return pl.BlockSpec((tm, w), lambda i, j: (i, 0))

    def full(a):
        return pl.BlockSpec(a.shape, lambda i, j: (0,) * a.ndim)

    return pl.pallas_call(
        functools.partial(_outffn_kernel, final_norm=final_norm),
        grid=(T // tm, Fd // tf),
        in_specs=[
            rows(D), rows(SB_WIDTH), rows(CA_WIDTH), rows(RW_WIDTH),
            full(prm["sb_g"]), full(prm["ca_g"]), full(prm["w_out"]), full(prm["ffn_g"]),
            pl.BlockSpec((D, tf), lambda i, j: (0, j)),
            pl.BlockSpec((tf, D), lambda i, j: (j, 0)),
            full(prm["final_g"]),
        ],
        out_specs=rows(D),
        out_shape=jax.ShapeDtypeStruct((T, D), F32),
        scratch_shapes=[pltpu.VMEM((tm, D), BF16), pltpu.VMEM((tm, D), F32)],
        compiler_params=_cparams("parallel", "arbitrary"),
        name="outproj_ffn",
    )(x2d, sb, ca, rw, prm["sb_g"], prm["ca_g"], prm["w_out"], prm["ffn_g"],
      prm["w_ff_in"], prm["w_ff_out"], prm["final_g"])


def _layer_params(l, p, ca_tq):
    f32 = F32
    w_in = p["w_in"][l]
    D = w_in.shape[0]
    attn = w_in[:, :COL_RW]
    rest = w_in[:, COL_RW:]
    o_w = RW_WIDTH
    o_k = o_w + D_DECAY_LORA
    o_v = o_k + RW_WIDTH
    o_a = o_v + RW_WIDTH
    o_g = o_a + D_AAA_LORA
    r_c, w_c, k_c, v_c = rest[:, :o_w], rest[:, o_w:o_k], rest[:, o_k:o_v], rest[:, o_v:o_a]
    a_c, g_c = rest[:, o_a:o_g], rest[:, o_g:]
    pad_gv = LORA_GV - D_GATE_LORA - D_MV_LORA
    if l == 0:
        vd_c = jnp.zeros((D, D_MV_LORA), f32)
    else:
        vd_c = p["w_vmix_down"][l - 1]
    w_proj = jnp.concatenate(
        [attn, r_c, k_c, v_c, w_c, a_c, g_c, vd_c, jnp.zeros((D, pad_gv), f32)], axis=1)
    mu = p["rw_mu"][l]
    mu_main = jnp.concatenate([mu[:o_w], mu[o_k:o_v], mu[o_v:o_a]])[None, :]
    mu_lora = jnp.concatenate(
        [mu[o_w:o_k], mu[o_a:o_g], mu[o_g:], jnp.zeros((D_MV_LORA + pad_gv,), f32)])[None, :]

    def padrows(w, before, total):
        return jnp.pad(w, ((before, total - before - w.shape[0]), (0, 0))).astype(BF16)

    prm = {
        "norm_g": p["norm_mix_g"][l][None, :],
        "w_proj": w_proj.astype(BF16),
        "mu_main": mu_main,
        "mu_lora": mu_lora,
        "w0": p["rw_w0"][l][None, :],
        "w2": padrows(p["rw_w2"][l], 0, LORA_WA),
        "a0": p["rw_a0"][l][None, :],
        "a2": padrows(p["rw_a2"][l], D_DECAY_LORA, LORA_WA),
        "g2": padrows(p["rw_g2"][l], 0, LORA_GV),
        "k_k": p["rw_k_k"][l][None, :],
        "k_a": p["rw_k_a"][l][None, :],
        "r_k": p["rw_r_k"][l][None, :],
        "ln_w": p["rw_ln_w"][l][None, :],
        "ln_b": p["rw_ln_b"][l][None, :],
        "sb_g": p["sb_out_g"][l][None, :],
        "ca_g": p["ca_out_g"][l][None, :],
        "w_out": p["w_out"][l].astype(BF16),
        "ffn_g": p["norm_ffn_g"][l][None, :],
        "w_ff_in": p["w_ff_in"][l].astype(BF16),
        "w_ff_out": p["w_ff_out"][l].astype(BF16),
        "final_g": p["norm_final_g"][None, :],
    }
    if l > 0:
        prm["v0"] = p["rw_v0"][l - 1][None, :]
        prm["v2"] = padrows(p["rw_v2"][l - 1], D_GATE_LORA, LORA_GV)
    table = p["ca_rel_bias"][l]
    left = CA_LEFT_CHUNKS * CHUNK + ca_tq - REL_CLIP
    right = 4 * ca_tq - left - table.shape[1]
    rel = jnp.pad(table, ((0, 0), (left, right)), mode="edge")
    prm["rel_rows"] = rel.reshape(CA_WIDTH // LANES, 2, 1, 4 * ca_tq)
    return prm


def _constants(sb_tq):
    idx = jnp.arange(sb_tq)
    tri_sb = (idx[:, None] > idx[None, :]).astype(BF16)
    c = jnp.arange(CHUNK)
    tri_chunk = (c[:, None] >= c[None, :]).astype(BF16)
    ln = jnp.arange(LANES)
    bd = ((ln[:, None] // HEAD_DIM) == (ln[None, :] // HEAD_DIM)).astype(BF16)
    return tri_sb, tri_chunk, bd


def _forward(x, p, *, tm_proj, sb_tq, ca_tq, tm_prep, tb_wkv, tm_ffn, tf_ffn):
    B, S, D = x.shape
    depth = p["w_in"].shape[0]
    tri_sb, tri_chunk, bd = _constants(sb_tq)
    x2d = x.reshape(B * S, D)
    v_first = None
    for l in range(depth):
        prm = _layer_params(l, p, ca_tq)
        prm["tri_chunk"] = tri_chunk
        prm["bd"] = bd
        proj = _inproj(x2d, prm["norm_g"], prm["w_proj"], tm_proj)
        sb = _sb_attention(proj, tri_sb, B, S, sb_tq)
        ca = _ca_attention(proj, prm["rel_rows"], B, S, ca_tq)
        feats = _rwprep(proj, v_first, prm, B, S, tm_prep, l == 0)
        if l == 0:
            v_first = feats[7]
        rw = _wkv(feats[:7], prm, B, S, tb_wkv)
        x2d = _outffn(x2d, sb, ca, rw, prm, tm_ffn, tf_ffn, l == depth - 1)
    return x2d.reshape(B, S, D)


def kernel(x, norm_mix_g, w_in, w_vmix_down, sb_out_g, ca_rel_bias, ca_out_g, rw_mu, rw_w0, rw_w2,
           rw_a0, rw_a2, rw_v0, rw_v2, rw_g2, rw_k_k, rw_k_a, rw_r_k, rw_ln_w, rw_ln_b, w_out,
           norm_ffn_g, w_ff_in, w_ff_out, norm_final_g):
    p = dict(norm_mix_g=norm_mix_g, w_in=w_in, w_vmix_down=w_vmix_down, sb_out_g=sb_out_g,
             ca_rel_bias=ca_rel_bias, ca_out_g=ca_out_g, rw_mu=rw_mu, rw_w0=rw_w0, rw_w2=rw_w2,
             rw_a0=rw_a0, rw_a2=rw_a2, rw_v0=rw_v0, rw_v2=rw_v2, rw_g2=rw_g2, rw_k_k=rw_k_k,
             rw_k_a=rw_k_a, rw_r_k=rw_r_k, rw_ln_w=rw_ln_w, rw_ln_b=rw_ln_b, w_out=w_out,
             norm_ffn_g=norm_ffn_g, w_ff_in=w_ff_in, w_ff_out=w_ff_out, norm_final_g=norm_final_g)
    return _forward(x, p, tm_proj=512, sb_tq=256, ca_tq=256, tm_prep=512, tb_wkv=512,
                    tm_ffn=512, tf_ffn=512)
```

```python
import functools

import jax
import jax.numpy as jnp
from jax import lax
from jax.experimental import pallas as pl
from jax.experimental.pallas import tpu as pltpu

F32 = jnp.float32
BF16 = jnp.bfloat16

LANES = 128
HEAD_DIM = 64
CHUNK = 64
SB_WIDTH = 256
CA_WIDTH = 256
RW_WIDTH = 512
CA_LEFT_CHUNKS = 8
REL_CLIP = 256
D_DECAY_LORA = 64
D_AAA_LORA = 64
D_GATE_LORA = 160
D_MV_LORA = 32
RMS_EPS = 1e-5
GN_EPS = 64e-5
MASK_VALUE = -1e30
SB_DEAD_LOG = -104.0
WKV_GROUP = 2

COL_SB = 0
COL_CA = 3 * SB_WIDTH
COL_RW = COL_CA + 3 * CA_WIDTH
LORA_WA = 2 * HEAD_DIM
LORA_GV = 256
COL_LORA = COL_RW + 3 * RW_WIDTH
N_PROJ = COL_LORA + LORA_WA + LORA_GV

VMEM_LIMIT = 56 * 1024 * 1024


def _cparams(*sem):
    return pltpu.CompilerParams(dimension_semantics=sem, vmem_limit_bytes=VMEM_LIMIT)


def _dot(a, b):
    return jnp.dot(a, b, preferred_element_type=F32)


def _dot_nt(a, b):
    return lax.dot_general(a, b, (((1,), (1,)), ((), ())), preferred_element_type=F32)


def _dot_tn(a, b):
    return lax.dot_general(a, b, (((0,), (0,)), ((), ())), preferred_element_type=F32)


def _split_dot(x, m):
    hi = x.astype(BF16)
    lo = (x - hi.astype(F32)).astype(BF16)
    return _dot(hi, m) + _dot(lo, m)


def _split_dot_left(m, x):
    hi = x.astype(BF16)
    lo = (x - hi.astype(F32)).astype(BF16)
    return _dot(m, hi) + _dot(m, lo)


def _softplus(x):
    return jnp.maximum(x, 0.0) + jnp.log(1.0 + jnp.exp(-jnp.abs(x)))


def _sigmoid(x):
    return 1.0 / (1.0 + jnp.exp(-x))


def _rms(x, g):
    return x * lax.rsqrt(jnp.mean(x * x, axis=-1, keepdims=True) + RMS_EPS) * g


def _inproj_kernel(x_ref, g_ref, w_ref, o_ref):
    h = _rms(x_ref[...], g_ref[...]).astype(BF16)
    o_ref[...] = _dot(h, w_ref[...])


def _inproj(x2d, g, w, tm):
    T, D = x2d.shape
    N = w.shape[1]
    return pl.pallas_call(
        _inproj_kernel,
        grid=(T // tm,),
        in_specs=[
            pl.BlockSpec((tm, D), lambda i: (i, 0)),
            pl.BlockSpec((1, D), lambda i: (0, 0)),
            pl.BlockSpec((D, N), lambda i: (0, 0)),
        ],
        out_specs=pl.BlockSpec((tm, N), lambda i: (i, 0)),
        out_shape=jax.ShapeDtypeStruct((T, N), F32),
        compiler_params=_cparams("parallel"),
        name="inproj",
    )(x2d, g, w)


def _sb_kernel(q_ref, k_ref, v_ref, tri_ref, o_ref, acc_ref, car_ref, *, tq):
    i = pl.program_id(1)
    n_tiles = SB_WIDTH // LANES
    H = range(2 * n_tiles)
    lane = lax.broadcasted_iota(jnp.int32, (tq, LANES), 1)
    head0 = lane < HEAD_DIM
    q = q_ref[...] * (HEAD_DIM ** -0.5)
    qh = []
    for t in range(n_tiles):
        qt = q[:, t * LANES:(t + 1) * LANES]
        qh += [jnp.where(head0, qt, 0.0).astype(BF16), jnp.where(head0, 0.0, qt).astype(BF16)]
    tri = tri_ref[...]
    strict = (lax.broadcasted_iota(jnp.int32, (tq, tq), 1)
              < lax.broadcasted_iota(jnp.int32, (tq, tq), 0))
    acc_ref[...] = jnp.zeros_like(acc_ref)
    car_ref[...] = jnp.zeros_like(car_ref)

    def tile(kb, diag):
        start = pl.multiple_of(kb * tq, tq)
        kblk = k_ref[pl.ds(start, tq), :].astype(BF16)
        vblk = v_ref[pl.ds(start, tq), :].astype(BF16)
        kt = [kblk[:, t * LANES:(t + 1) * LANES] for t in range(n_tiles)]
        vt = [vblk[:, t * LANES:(t + 1) * LANES] for t in range(n_tiles)]
        z = [_dot_nt(qh[h], kt[h // 2]) for h in H]
        sp = [_softplus(z[h]) for h in H]
        log_1m = [-sp[h] for h in H]
        if diag:
            log_1m = [jnp.where(strict, log_1m[h], 0.0) for h in H]
        after = [_split_dot(log_1m[h], tri) for h in H]
        carry = [car_ref[h] for h in H]
        w = [jnp.exp((z[h] - sp[h]) + after[h] + carry[h]) for h in H]
        if diag:
            w = [jnp.where(strict, w[h], 0.0) for h in H]
        pv = [_dot(w[h].astype(BF16), vt[h // 2]) for h in H]
        for t in range(n_tiles):
            acc_ref[t] += jnp.where(head0, pv[2 * t], pv[2 * t + 1])
        for h in H:
            car_ref[h] = carry[h] + after[h][:, :1] + log_1m[h][:, :1]

    tile(i, True)

    def live():
        return jnp.max(car_ref[...]) > SB_DEAD_LOG

    def cond(s):
        return (s[0] < i) & s[1]

    def body(s):
        tile(i - 1 - s[0], False)
        return s[0] + 1, live()

    lax.while_loop(cond, body, (jnp.int32(0), live()))
    o_ref[...] = jnp.concatenate([acc_ref[t] for t in range(n_tiles)], axis=1)


def _sb_attention(proj, tri, B, S, tq):
    T = B * S
    nq = S // tq
    assert COL_SB == 0
    return pl.pallas_call(
        functools.partial(_sb_kernel, tq=tq),
        grid=(B, nq),
        in_specs=[
            pl.BlockSpec((tq, SB_WIDTH), lambda b, i: (b * nq + i, 0)),
            pl.BlockSpec((S, SB_WIDTH), lambda b, i: (b, 1)),
            pl.BlockSpec((S, SB_WIDTH), lambda b, i: (b, 2)),
            pl.BlockSpec((tq, tq), lambda b, i: (0, 0)),
        ],
        out_specs=pl.BlockSpec((tq, SB_WIDTH), lambda b, i: (b * nq + i, 0)),
        out_shape=jax.ShapeDtypeStruct((T, SB_WIDTH), F32),
        scratch_shapes=[pltpu.VMEM((SB_WIDTH // LANES, tq, LANES), F32),
                        pltpu.VMEM((SB_WIDTH // HEAD_DIM, tq, 1), F32)],
        compiler_params=_cparams("parallel", "arbitrary"),
        name="sb_attention",
    )(proj, proj, proj, tri)


def _ca_kernel(q_ref, k0_ref, k1_ref, k2_ref, v0_ref, v1_ref, v2_ref, rel_ref, o_ref, bias_ref,
               *, tq):
    i = pl.program_id(2)
    nk = 3 * tq
    left = CA_LEFT_CHUNKS * CHUNK

    @pl.when(i == 0)
    def _():
        qi = lax.broadcasted_iota(jnp.int32, (tq, nk), 0)
        kj = lax.broadcasted_iota(jnp.int32, (tq, nk), 1)
        qc = qi // CHUNK
        kc = kj // CHUNK
        band = (kc >= qc) & (kc <= qc + CA_LEFT_CHUNKS)
        for h in range(2):
            row = jnp.broadcast_to(rel_ref[h], (tq, rel_ref.shape[-1]))
            toep = pltpu.roll(row, rel_ref.shape[-1] - tq, 1, stride=1, stride_axis=0)
            bias_ref[h] = jnp.where(band, toep[:, :nk], MASK_VALUE)

    lane = lax.broadcasted_iota(jnp.int32, (tq, LANES), 1)
    head0 = lane < HEAD_DIM
    q = q_ref[...] * (HEAD_DIM ** -0.5)
    k = jnp.concatenate([k0_ref[...], k1_ref[...], k2_ref[...]], axis=0).astype(BF16)
    v = jnp.concatenate([v0_ref[...], v1_ref[...], v2_ref[...]], axis=0).astype(BF16)
    key_abs = lax.broadcasted_iota(jnp.int32, (tq, nk), 1) + (i * tq - left)
    valid = key_abs >= 0
    outs = []
    for h in range(2):
        qh = (jnp.where(head0, q, 0.0) if h == 0 else jnp.where(head0, 0.0, q)).astype(BF16)
        s = _dot_nt(qh, k) + bias_ref[h]
        s = jnp.where(valid, s, MASK_VALUE)
        m = jnp.max(s, axis=-1, keepdims=True)
        e = jnp.exp(s - m)
        l = jnp.sum(e, axis=-1, keepdims=True)
        outs.append(_dot(e.astype(BF16), v) / l)
    o_ref[...] = jnp.where(head0, outs[0], outs[1])


def _ca_attention(proj, rel_rows, B, S, tq):
    T = B * S
    nq = S // tq
    assert left_blocks(tq) == 2
    qcol = COL_CA // LANES
    kcol = (COL_CA + CA_WIDTH) // LANES
    vcol = (COL_CA + 2 * CA_WIDTH) // LANES

    def kv_spec(col, back):
        return pl.BlockSpec(
            (tq, LANES), lambda b, p, i: (b * nq + jnp.maximum(i - back, 0), col + p))

    return pl.pallas_call(
        functools.partial(_ca_kernel, tq=tq),
        grid=(B, CA_WIDTH // LANES, nq),
        in_specs=[
            pl.BlockSpec((tq, LANES), lambda b, p, i: (b * nq + i, qcol + p)),
            kv_spec(kcol, 2), kv_spec(kcol, 1), kv_spec(kcol, 0),
            kv_spec(vcol, 2), kv_spec(vcol, 1), kv_spec(vcol, 0),
            pl.BlockSpec((None, 2, 1, 4 * tq), lambda b, p, i: (p, 0, 0, 0)),
        ],
        out_specs=pl.BlockSpec((tq, LANES), lambda b, p, i: (b * nq + i, p)),
        out_shape=jax.ShapeDtypeStruct((T, CA_WIDTH), F32),
        scratch_shapes=[pltpu.VMEM((2, tq, 3 * tq), F32)],
        compiler_params=_cparams("parallel", "parallel", "arbitrary"),
        name="ca_attention",
    )(proj, proj, proj, proj, proj, proj, proj, rel_rows)


def left_blocks(tq):
    return (CA_LEFT_CHUNKS * CHUNK) // tq


def _seg_sum(x, bd):
    parts = [_split_dot(x[:, c:c + LANES], bd) for c in range(0, x.shape[1], LANES)]
    return jnp.concatenate(parts, axis=1)


def _rwprep_kernel(*refs, tm, tiles_per_seq, first_layer):
    if first_layer:
        (main_ref, lora_ref, pmain_ref, plora_ref, mum_ref, mul_ref, w0_ref, w2_ref, a0_ref,
         a2_ref, g2_ref, kkw_ref, kaw_ref, bd_ref,
         r_out, lw_out, k_out, v_out, kk_out, a_out, g_out, vf_out) = refs
    else:
        (main_ref, lora_ref, pmain_ref, plora_ref, vf_ref, mum_ref, mul_ref, w0_ref, w2_ref,
         a0_ref, a2_ref, v0_ref, v2_ref, g2_ref, kkw_ref, kaw_ref, bd_ref,
         r_out, lw_out, k_out, v_out, kk_out, a_out, g_out) = refs
    i = pl.program_id(0)
    seq_start = (i % tiles_per_seq) == 0

    def shift_lerp(x, prev8, mu):
        prev = jnp.where(seq_start, 0.0, prev8[7:8, :])
        row = lax.broadcasted_iota(jnp.int32, x.shape, 0)
        xs = jnp.where(row == 0, prev, pltpu.roll(x, 1, 0))
        return x + (xs - x) * mu

    cm = shift_lerp(main_ref[...], pmain_ref[...], mum_ref[...])
    cl = shift_lerp(lora_ref[...], plora_ref[...], mul_ref[...])
    r = cm[:, :RW_WIDTH]
    k = cm[:, RW_WIDTH:2 * RW_WIDTH]
    v = cm[:, 2 * RW_WIDTH:]
    wa = cl[:, :LORA_WA]
    gv = cl[:, LORA_WA:]

    log_w = -_softplus(-(w0_ref[...] + _dot(jnp.tanh(wa).astype(BF16), w2_ref[...]))) - 0.5
    lw_out[...] = -jnp.exp(log_w)
    a = _sigmoid(a0_ref[...] + _dot(wa.astype(BF16), a2_ref[...]))
    g_out[...] = _dot(_sigmoid(gv).astype(BF16), g2_ref[...])
    if first_layer:
        vf_out[...] = v
    else:
        mix = _sigmoid(v0_ref[...] + _dot(gv.astype(BF16), v2_ref[...]))
        v = v + (vf_ref[...] - v) * mix
    kk = k * kkw_ref[...]
    norm = jnp.sqrt(_seg_sum(kk * kk, bd_ref[...]))
    kk_out[...] = kk / jnp.maximum(norm, 1e-12)
    k_out[...] = k * (1.0 + (a - 1.0) * kaw_ref[...])
    r_out[...] = r
    v_out[...] = v
    a_out[...] = a


def _rwprep(proj, v_first, prm, B, S, tm, first_layer):
    T = B * S
    nt = T // tm
    sub = tm // 8
    main_blk = COL_RW // (3 * RW_WIDTH)
    lora_blk = COL_LORA // (LORA_WA + LORA_GV)
    assert main_blk * 3 * RW_WIDTH == COL_RW and lora_blk * (LORA_WA + LORA_GV) == COL_LORA
    wl = LORA_WA + LORA_GV

    def full(a):
        return pl.BlockSpec(a.shape, lambda i: (0,) * a.ndim)

    row = pl.BlockSpec((tm, RW_WIDTH), lambda i: (i, 0))
    in_specs = [
        pl.BlockSpec((tm, 3 * RW_WIDTH), lambda i: (i, main_blk)),
        pl.BlockSpec((tm, wl), lambda i: (i, lora_blk)),
        pl.BlockSpec((8, 3 * RW_WIDTH), lambda i: (jnp.maximum(i * sub - 1, 0), main_blk)),
        pl.BlockSpec((8, wl), lambda i: (jnp.maximum(i * sub - 1, 0), lora_blk)),
    ]
    args = [proj, proj, proj, proj]
    if not first_layer:
        in_specs.append(row)
        args.append(v_first)
    names = ["mu_main", "mu_lora", "w0", "w2", "a0", "a2"]
    if not first_layer:
        names += ["v0", "v2"]
    names += ["g2", "k_k", "k_a", "bd"]
    for n in names:
        in_specs.append(full(prm[n]))
        args.append(prm[n])
    n_out = 8 if first_layer else 7
    outs = pl.pallas_call(
        functools.partial(_rwprep_kernel, tm=tm, tiles_per_seq=S // tm, first_layer=first_layer),
        grid=(nt,),
        in_specs=in_specs,
        out_specs=[row] * n_out,
        out_shape=[jax.ShapeDtypeStruct((T, RW_WIDTH), F32)] * n_out,
        compiler_params=_cparams("parallel"),
        name="rwkv_prep",
    )(*args)
    return outs


def _wkv_kernel(r_ref, lw_ref, k_ref, v_ref, kk_ref, a_ref, g_ref, rk_ref, lnw_ref, lnb_ref,
                tri_ref, bd_ref, o_ref, state_ref, y_ref, *, tb):
    C = CHUNK
    n_pairs = RW_WIDTH // LANES

    @pl.when(pl.program_id(1) == 0)
    def _():
        state_ref[...] = jnp.zeros_like(state_ref)

    lane = lax.broadcasted_iota(jnp.int32, (C, LANES), 1)
    head0 = lane < HEAD_DIM
    rows2 = lax.broadcasted_iota(jnp.int32, (2 * C, 2 * C), 0)
    cols2 = lax.broadcasted_iota(jnp.int32, (2 * C, 2 * C), 1)
    same = (rows2 // C) == (cols2 // C)
    strict = same & (cols2 < rows2)
    incl = same & (cols2 <= rows2)
    eye = (rows2 == cols2).astype(F32)
    blk_xor = rows2 ^ cols2
    tri = tri_ref[...]

    def stack(x):
        return jnp.concatenate([jnp.where(head0, x, 0.0), jnp.where(head0, 0.0, x)], axis=0)

    def prepare(g):
        units = [(dc, p) for dc in range(WKV_GROUP) for p in range(n_pairs)]
        P = range(len(units))

        def load(ref):
            return [ref[pl.ds(pl.multiple_of((g * WKV_GROUP + dc) * C, C), C),
                        p * LANES:(p + 1) * LANES] for dc, p in units]

        r, lw, k, v, kk, a = (load(x) for x in (r_ref, lw_ref, k_ref, v_ref, kk_ref, a_ref))
        cl = [_split_dot_left(tri, lw[p]) for p in P]
        e_pos = [jnp.exp(cl[p]) for p in P]
        e_neg = [jnp.exp(-cl[p]) for p in P]
        rt = [stack(r[p] * e_pos[p]).astype(BF16) for p in P]
        at = [stack(-kk[p] * jnp.exp(cl[p] - lw[p])).astype(BF16) for p in P]
        bt = [stack(kk[p] * a[p] * e_neg[p]).astype(BF16) for p in P]
        kt = [stack(k[p] * e_neg[p]).astype(BF16) for p in P]
        vs = [stack(v[p]).astype(BF16) for p in P]
        prod = [_dot_nt(jnp.concatenate([at[p], rt[p]], axis=0),
                        jnp.concatenate([bt[p], kt[p]], axis=0)) for p in P]
        l_ab = [jnp.where(strict, prod[p][:2 * C, :2 * C], 0.0) for p in P]
        l_ak = [jnp.where(strict, prod[p][:2 * C, 2 * C:], 0.0).astype(BF16) for p in P]
        q_rb = [jnp.where(incl, prod[p][2 * C:, :2 * C], 0.0).astype(BF16) for p in P]
        q_rk = [jnp.where(incl, prod[p][2 * C:, 2 * C:], 0.0).astype(BF16) for p in P]
        lakv = [_dot(l_ak[p], vs[p]).astype(BF16) for p in P]
        y0 = [_dot(q_rk[p], vs[p]) for p in P]
        kv = [_dot_tn(vs[p], kt[p]) for p in P]

        pm = [eye + jnp.where(blk_xor == 1, l_ab[p], 0.0) for p in P]
        s = 2
        while s < C:
            level = (blk_xor >= s) & (blk_xor < 2 * s)
            e = [jnp.where(level, l_ab[p], 0.0).astype(BF16) for p in P]
            pb = [pm[p].astype(BF16) for p in P]
            et = [_dot(e[p], pb[p]).astype(BF16) for p in P]
            pm = [pm[p] + _dot(pb[p], et[p]) for p in P]
            s *= 2
        tw = [_dot(pm[p].astype(BF16), jnp.concatenate([at[p], lakv[p]], axis=1)) for p in P]
        return tuple(
            (jnp.concatenate([tw[p][:, :LANES].astype(BF16), rt[p]], axis=0), tw[p][:, LANES:],
             q_rb[p], y0[p][:C] + y0[p][C:], bt[p], kv[p], e_pos[p][C - 1:C, :]) for p in P)

    def advance(g, prepared):
        P = range(n_pairs)
        states = [state_ref[p] for p in P]
        outs = []
        for dc in range(WKV_GROUP):
            w_r, u0, q_rb, y0, bt, kv, decay = zip(*prepared[dc * n_pairs:(dc + 1) * n_pairs])
            from_state = [_dot_nt(w_r[p], states[p].astype(BF16)) for p in P]
            ub = [(from_state[p][:2 * C] + u0[p]).astype(BF16) for p in P]
            states = [(states[p] + _dot_tn(ub[p], bt[p]) + kv[p]) * decay[p] for p in P]
            y2 = [from_state[p][2 * C:] + _dot(q_rb[p], ub[p]) for p in P]
            outs.append([y0[p] + y2[p][:C] + y2[p][C:] for p in P])
        for p in P:
            state_ref[p] = states[p]
        for dc in range(WKV_GROUP):
            rows = pl.ds(pl.multiple_of((g * WKV_GROUP + dc) * C, C), C)
            for p in P:
                y_ref[rows, p * LANES:(p + 1) * LANES] = outs[dc][p]

    n_groups = tb // (C * WKV_GROUP)

    def body(g, prepared):
        nxt = prepare(g)
        advance(g - 1, prepared)
        return nxt

    last = lax.fori_loop(1, n_groups, body, prepare(0))
    advance(n_groups - 1, last)

    bd = bd_ref[...]
    y = y_ref[...]
    mean = _seg_sum(y, bd) * (1.0 / HEAD_DIM)
    d = y - mean
    var = _seg_sum(d * d, bd) * (1.0 / HEAD_DIM)
    yn = d * lax.rsqrt(var + GN_EPS) * lnw_ref[...] + lnb_ref[...]
    bonus = _seg_sum(r_ref[...] * k_ref[...] * rk_ref[...], bd) * v_ref[...]
    o_ref[...] = (yn + bonus) * g_ref[...]


def _wkv(feats, prm, B, S, tb):
    T = B * S
    nt = S // tb
    r, lw, k, v, kk, a, g = feats
    row = pl.BlockSpec((tb, RW_WIDTH), lambda b, t: (b * nt + t, 0))

    def full(x):
        return pl.BlockSpec(x.shape, lambda b, t: (0,) * x.ndim)

    small = [prm["r_k"], prm["ln_w"], prm["ln_b"], prm["tri_chunk"], prm["bd"]]
    return pl.pallas_call(
        functools.partial(_wkv_kernel, tb=tb),
        grid=(B, nt),
        in_specs=[row] * 7 + [full(x) for x in small],
        out_specs=row,
        out_shape=jax.ShapeDtypeStruct((T, RW_WIDTH), F32),
        scratch_shapes=[pltpu.VMEM((RW_WIDTH // LANES, LANES, LANES), F32),
                        pltpu.VMEM((tb, RW_WIDTH), F32)],
        compiler_params=_cparams("parallel", "arbitrary"),
        name="wkv7",
    )(r, lw, k, v, kk, a, g, *small)


def _outffn_kernel(x_ref, sb_ref, ca_ref, rw_ref, sbg_ref, cag_ref, wo_ref, fg_ref, w1_ref, w2_ref,
                   fin_ref, o_ref, h_ref, acc_ref, *, final_norm):
    j = pl.program_id(1)

    @pl.when(j == 0)
    def _():
        sbn = _rms(sb_ref[...], sbg_ref[...]).astype(BF16)
        can = _rms(ca_ref[...], cag_ref[...]).astype(BF16)
        x = x_ref[...]
        x = x + _dot(sbn, wo_ref[:SB_WIDTH, :])
        x = x + _dot(can, wo_ref[SB_WIDTH:SB_WIDTH + CA_WIDTH, :])
        x = x + _dot(rw_ref[...].astype(BF16), wo_ref[SB_WIDTH + CA_WIDTH:, :])
        acc_ref[...] = x
        h_ref[...] = _rms(x, fg_ref[...]).astype(BF16)

    f = jnp.square(jnp.maximum(_dot(h_ref[...], w1_ref[...]), 0.0)).astype(BF16)
    acc_ref[...] += _dot(f, w2_ref[...])

    @pl.when(j == pl.num_programs(1) - 1)
    def _():
        y = acc_ref[...]
        if final_norm:
            y = _rms(y, fin_ref[...])
        o_ref[...] = y


def _outffn(x2d, sb, ca, rw, prm, tm, tf, final_norm):
    T, D = x2d.shape
    Fd = prm["w_ff_in"].shape[1]

    def rows(w):
        return pl.BlockSpec((tm, w), lambda i, j: (i, 0))

    def full(a):
        return pl.BlockSpec(a.shape, lambda i, j: (0,) * a.ndim)

    return pl.pallas_call(
        functools.partial(_outffn_kernel, final_norm=final_norm),
        grid=(T // tm, Fd // tf),
        in_specs=[
            rows(D), rows(SB_WIDTH), rows(CA_WIDTH), rows(RW_WIDTH),
            full(prm["sb_g"]), full(prm["ca_g"]), full(prm["w_out"]), full(prm["ffn_g"]),
            pl.BlockSpec((D, tf), lambda i, j: (0, j)),
            pl.BlockSpec((tf, D), lambda i, j: (j, 0)),
            full(prm["final_g"]),
        ],
        out_specs=rows(D),
        out_shape=jax.ShapeDtypeStruct((T, D), F32),
        scratch_shapes=[pltpu.VMEM((tm, D), BF16), pltpu.VMEM((tm, D), F32)],
        compiler_params=_cparams("parallel", "arbitrary"),
        name="outproj_ffn",
    )(x2d, sb, ca, rw, prm["sb_g"], prm["ca_g"], prm["w_out"], prm["ffn_g"],
      prm["w_ff_in"], prm["w_ff_out"], prm["final_g"])


def _layer_params(l, p, ca_tq):
    f32 = F32
    w_in = p["w_in"][l]
    D = w_in.shape[0]
    attn = w_in[:, :COL_RW]
    rest = w_in[:, COL_RW:]
    o_w = RW_WIDTH
    o_k = o_w + D_DECAY_LORA
    o_v = o_k + RW_WIDTH
    o_a = o_v + RW_WIDTH
    o_g = o_a + D_AAA_LORA
    r_c, w_c, k_c, v_c = rest[:, :o_w], rest[:, o_w:o_k], rest[:, o_k:o_v], rest[:, o_v:o_a]
    a_c, g_c = rest[:, o_a:o_g], rest[:, o_g:]
    pad_gv = LORA_GV - D_GATE_LORA - D_MV_LORA
    if l == 0:
        vd_c = jnp.zeros((D, D_MV_LORA), w_in.dtype)
    else:
        vd_c = p["w_vmix_down"][l - 1]
    w_proj = jnp.concatenate(
        [attn, r_c, k_c, v_c, w_c, a_c, g_c, vd_c, jnp.zeros((D, pad_gv), w_in.dtype)], axis=1)
    mu = p["rw_mu"][l]
    mu_main = jnp.concatenate([mu[:o_w], mu[o_k:o_v], mu[o_v:o_a]])[None, :]
    mu_lora = jnp.concatenate(
        [mu[o_w:o_k], mu[o_a:o_g], mu[o_g:], jnp.zeros((D_MV_LORA + pad_gv,), f32)])[None, :]

    def padrows(w, before, total):
        return jnp.pad(w, ((before, total - before - w.shape[0]), (0, 0))).astype(BF16)

    prm = {
        "norm_g": p["norm_mix_g"][l][None, :],
        "w_proj": w_proj.astype(BF16),
        "mu_main": mu_main,
        "mu_lora": mu_lora,
        "w0": p["rw_w0"][l][None, :],
        "w2": padrows(p["rw_w2"][l], 0, LORA_WA),
        "a0": p["rw_a0"][l][None, :],
        "a2": padrows(p["rw_a2"][l], D_DECAY_LORA, LORA_WA),
        "g2": padrows(p["rw_g2"][l], 0, LORA_GV),
        "k_k": p["rw_k_k"][l][None, :],
        "k_a": p["rw_k_a"][l][None, :],
        "r_k": p["rw_r_k"][l][None, :],
        "ln_w": p["rw_ln_w"][l][None, :],
        "ln_b": p["rw_ln_b"][l][None, :],
        "sb_g": p["sb_out_g"][l][None, :],
        "ca_g": p["ca_out_g"][l][None, :],
        "w_out": p["w_out"][l].astype(BF16),
        "ffn_g": p["norm_ffn_g"][l][None, :],
        "w_ff_in": p["w_ff_in"][l].astype(BF16),
        "w_ff_out": p["w_ff_out"][l].astype(BF16),
        "final_g": p["norm_final_g"][None, :],
    }
    if l > 0:
        prm["v0"] = p["rw_v0"][l - 1][None, :]
        prm["v2"] = padrows(p["rw_v2"][l - 1], D_GATE_LORA, LORA_GV)
    table = p["ca_rel_bias"][l]
    left = CA_LEFT_CHUNKS * CHUNK + ca_tq - REL_CLIP
    right = 4 * ca_tq - left - table.shape[1]
    rel = jnp.pad(table, ((0, 0), (left, right)), mode="edge")
    prm["rel_rows"] = rel.reshape(CA_WIDTH // LANES, 2, 1, 4 * ca_tq)
    return prm


def _constants(sb_tq):
    idx = jnp.arange(sb_tq)
    tri_sb = (idx[:, None] > idx[None, :]).astype(BF16)
    c = jnp.arange(CHUNK)
    tri_chunk = (c[:, None] >= c[None, :]).astype(BF16)
    ln = jnp.arange(LANES)
    bd = ((ln[:, None] // HEAD_DIM) == (ln[None, :] // HEAD_DIM)).astype(BF16)
    return tri_sb, tri_chunk, bd


def _forward(x, p, *, tm_proj, sb_tq, ca_tq, tm_prep, tb_wkv, tm_ffn, tf_ffn):
    B, S, D = x.shape
    depth = p["w_in"].shape[0]
    p = dict(p)
    for name in ("w_in", "w_vmix_down", "w_out", "w_ff_in", "w_ff_out"):
        p[name] = p[name].astype(BF16)
    tri_sb, tri_chunk, bd = _constants(sb_tq)
    x2d = x.reshape(B * S, D)
    v_first = None
    for l in range(depth):
        prm = _layer_params(l, p, ca_tq)
        prm["tri_chunk"] = tri_chunk
        prm["bd"] = bd
        proj = _inproj(x2d, prm["norm_g"], prm["w_proj"], tm_proj)
        sb = _sb_attention(proj, tri_sb, B, S, sb_tq)
        ca = _ca_attention(proj, prm["rel_rows"], B, S, ca_tq)
        feats = _rwprep(proj, v_first, prm, B, S, tm_prep, l == 0)
        if l == 0:
            v_first = feats[7]
        rw = _wkv(feats[:7], prm, B, S, tb_wkv)
        x2d = _outffn(x2d, sb, ca, rw, prm, tm_ffn, tf_ffn, l == depth - 1)
    return x2d.reshape(B, S, D)


def kernel(x, norm_mix_g, w_in, w_vmix_down, sb_out_g, ca_rel_bias, ca_out_g, rw_mu, rw_w0, rw_w2,
           rw_a0, rw_a2, rw_v0, rw_v2, rw_g2, rw_k_k, rw_k_a, rw_r_k, rw_ln_w, rw_ln_b, w_out,
           norm_ffn_g, w_ff_in, w_ff_out, norm_final_g):
    p = dict(norm_mix_g=norm_mix_g, w_in=w_in, w_vmix_down=w_vmix_down, sb_out_g=sb_out_g,
             ca_rel_bias=ca_rel_bias, ca_out_g=ca_out_g, rw_mu=rw_mu, rw_w0=rw_w0, rw_w2=rw_w2,
             rw_a0=rw_a0, rw_a2=rw_a2, rw_v0=rw_v0, rw_v2=rw_v2, rw_g2=rw_g2, rw_k_k=rw_k_k,
             rw_k_a=rw_k_a, rw_r_k=rw_r_k, rw_ln_w=rw_ln_w, rw_ln_b=rw_ln_b, w_out=w_out,
             norm_ffn_g=norm_ffn_g, w_ff_in=w_ff_in, w_ff_out=w_ff_out, norm_final_g=norm_final_g)
    return _forward(x, p, tm_proj=512, sb_tq=256, ca_tq=256, tm_prep=512, tb_wkv=1024,
                    tm_ffn=1024, tf_ffn=512)
```

```python
import functools

import jax
import jax.numpy as jnp
from jax import lax
from jax.experimental import pallas as pl
from jax.experimental.pallas import tpu as pltpu

F32 = jnp.float32
BF16 = jnp.bfloat16

LANES = 128
HEAD_DIM = 64
CHUNK = 64
SB_WIDTH = 256
CA_WIDTH = 256
RW_WIDTH = 512
CA_LEFT_CHUNKS = 8
REL_CLIP = 256
D_DECAY_LORA = 64
D_AAA_LORA = 64
D_GATE_LORA = 160
D_MV_LORA = 32
RMS_EPS = 1e-5
GN_EPS = 64e-5
MASK_VALUE = -1e30
SB_DEAD_LOG = -104.0
WKV_GROUP = 4

COL_SB = 0
COL_CA = 3 * SB_WIDTH
COL_RW = COL_CA + 3 * CA_WIDTH
LORA_WA = 2 * HEAD_DIM
LORA_GV = 256
COL_LORA = COL_RW + 3 * RW_WIDTH
N_PROJ = COL_LORA + LORA_WA + LORA_GV

VMEM_LIMIT = 56 * 1024 * 1024


def _cparams(*sem):
    return pltpu.CompilerParams(dimension_semantics=sem, vmem_limit_bytes=VMEM_LIMIT)


def _dot(a, b):
    return jnp.dot(a, b, preferred_element_type=F32)


def _dot_nt(a, b):
    return lax.dot_general(a, b, (((1,), (1,)), ((), ())), preferred_element_type=F32)


def _dot_tn(a, b):
    return lax.dot_general(a, b, (((0,), (0,)), ((), ())), preferred_element_type=F32)


def _split_dot(x, m):
    hi = x.astype(BF16)
    lo = (x - hi.astype(F32)).astype(BF16)
    return _dot(hi, m) + _dot(lo, m)


def _split_dot_left(m, x):
    hi = x.astype(BF16)
    lo = (x - hi.astype(F32)).astype(BF16)
    return _dot(m, hi) + _dot(m, lo)


def _softplus(x):
    return jnp.maximum(x, 0.0) + jnp.log(1.0 + jnp.exp(-jnp.abs(x)))


def _sigmoid(x):
    return 1.0 / (1.0 + jnp.exp(-x))


def _rms(x, g):
    return x * lax.rsqrt(jnp.mean(x * x, axis=-1, keepdims=True) + RMS_EPS) * g


def _inproj_kernel(x_ref, g_ref, w_ref, o_ref):
    h = _rms(x_ref[...], g_ref[...]).astype(BF16)
    o_ref[...] = _dot(h, w_ref[...])


def _inproj(x2d, g, w, tm):
    T, D = x2d.shape
    N = w.shape[1]
    return pl.pallas_call(
        _inproj_kernel,
        grid=(T // tm,),
        in_specs=[
            pl.BlockSpec((tm, D), lambda i: (i, 0)),
            pl.BlockSpec((1, D), lambda i: (0, 0)),
            pl.BlockSpec((D, N), lambda i: (0, 0)),
        ],
        out_specs=pl.BlockSpec((tm, N), lambda i: (i, 0)),
        out_shape=jax.ShapeDtypeStruct((T, N), F32),
        compiler_params=_cparams("parallel"),
        name="inproj",
    )(x2d, g, w)


def _sb_kernel(q_ref, k_ref, v_ref, tri_ref, o_ref, acc_ref, car_ref, *, tq):
    i = pl.program_id(1)
    n_tiles = SB_WIDTH // LANES
    H = range(2 * n_tiles)
    lane = lax.broadcasted_iota(jnp.int32, (tq, LANES), 1)
    head0 = lane < HEAD_DIM
    q = q_ref[...] * (HEAD_DIM ** -0.5)
    qh = []
    for t in range(n_tiles):
        qt = q[:, t * LANES:(t + 1) * LANES]
        qh += [jnp.where(head0, qt, 0.0).astype(BF16), jnp.where(head0, 0.0, qt).astype(BF16)]
    tri = tri_ref[...]
    strict = (lax.broadcasted_iota(jnp.int32, (tq, tq), 1)
              < lax.broadcasted_iota(jnp.int32, (tq, tq), 0))
    acc_ref[...] = jnp.zeros_like(acc_ref)
    car_ref[...] = jnp.zeros_like(car_ref)

    def tile(kb, diag):
        start = pl.multiple_of(kb * tq, tq)
        kblk = k_ref[pl.ds(start, tq), :].astype(BF16)
        vblk = v_ref[pl.ds(start, tq), :].astype(BF16)
        kt = [kblk[:, t * LANES:(t + 1) * LANES] for t in range(n_tiles)]
        vt = [vblk[:, t * LANES:(t + 1) * LANES] for t in range(n_tiles)]
        z = [_dot_nt(qh[h], kt[h // 2]) for h in H]
        sp = [_softplus(z[h]) for h in H]
        log_1m = [-sp[h] for h in H]
        if diag:
            log_1m = [jnp.where(strict, log_1m[h], 0.0) for h in H]
        after = [_split_dot(log_1m[h], tri) for h in H]
        carry = [car_ref[h] for h in H]
        w = [jnp.exp((z[h] - sp[h]) + after[h] + carry[h]) for h in H]
        if diag:
            w = [jnp.where(strict, w[h], 0.0) for h in H]
        pv = [_dot(w[h].astype(BF16), vt[h // 2]) for h in H]
        for t in range(n_tiles):
            acc_ref[t] += jnp.where(head0, pv[2 * t], pv[2 * t + 1])
        for h in H:
            car_ref[h] = carry[h] + after[h][:, :1] + log_1m[h][:, :1]

    tile(i, True)

    def live():
        return jnp.max(car_ref[...]) > SB_DEAD_LOG

    def cond(s):
        return (s[0] < i) & s[1]

    def body(s):
        tile(i - 1 - s[0], False)
        return s[0] + 1, live()

    lax.while_loop(cond, body, (jnp.int32(0), live()))
    o_ref[...] = jnp.concatenate([acc_ref[t] for t in range(n_tiles)], axis=1)


def _sb_attention(proj, tri, B, S, tq):
    T = B * S
    nq = S // tq
    assert COL_SB == 0
    return pl.pallas_call(
        functools.partial(_sb_kernel, tq=tq),
        grid=(B, nq),
        in_specs=[
            pl.BlockSpec((tq, SB_WIDTH), lambda b, i: (b * nq + i, 0)),
            pl.BlockSpec((S, SB_WIDTH), lambda b, i: (b, 1)),
            pl.BlockSpec((S, SB_WIDTH), lambda b, i: (b, 2)),
            pl.BlockSpec((tq, tq), lambda b, i: (0, 0)),
        ],
        out_specs=pl.BlockSpec((tq, SB_WIDTH), lambda b, i: (b * nq + i, 0)),
        out_shape=jax.ShapeDtypeStruct((T, SB_WIDTH), F32),
        scratch_shapes=[pltpu.VMEM((SB_WIDTH // LANES, tq, LANES), F32),
                        pltpu.VMEM((SB_WIDTH // HEAD_DIM, tq, 1), F32)],
        compiler_params=_cparams("parallel", "arbitrary"),
        name="sb_attention",
    )(proj, proj, proj, tri)


def _ca_kernel(q_ref, k0_ref, k1_ref, k2_ref, v0_ref, v1_ref, v2_ref, rel_ref, o_ref, bias_ref,
               *, tq):
    i = pl.program_id(1)
    nk = 3 * tq
    left = CA_LEFT_CHUNKS * CHUNK
    n_tiles = CA_WIDTH // LANES
    H = range(2 * n_tiles)

    @pl.when(i == 0)
    def _():
        qi = lax.broadcasted_iota(jnp.int32, (tq, nk), 0)
        kj = lax.broadcasted_iota(jnp.int32, (tq, nk), 1)
        qc = qi // CHUNK
        kc = kj // CHUNK
        band = (kc >= qc) & (kc <= qc + CA_LEFT_CHUNKS)
        for h in H:
            row = jnp.broadcast_to(rel_ref[h], (tq, rel_ref.shape[-1]))
            toep = pltpu.roll(row, rel_ref.shape[-1] - tq, 1, stride=1, stride_axis=0)
            bias_ref[h] = jnp.where(band, toep[:, :nk], MASK_VALUE)

    lane = lax.broadcasted_iota(jnp.int32, (tq, LANES), 1)
    head0 = lane < HEAD_DIM
    q = q_ref[...] * (HEAD_DIM ** -0.5)
    k = jnp.concatenate([k0_ref[...], k1_ref[...], k2_ref[...]], axis=0).astype(BF16)
    v = jnp.concatenate([v0_ref[...], v1_ref[...], v2_ref[...]], axis=0).astype(BF16)
    key_abs = lax.broadcasted_iota(jnp.int32, (tq, nk), 1) + (i * tq - left)
    valid = key_abs >= 0
    qh = []
    for t in range(n_tiles):
        qt = q[:, t * LANES:(t + 1) * LANES]
        qh += [jnp.where(head0, qt, 0.0).astype(BF16), jnp.where(head0, 0.0, qt).astype(BF16)]
    kt = [k[:, t * LANES:(t + 1) * LANES] for t in range(n_tiles)]
    vt = [v[:, t * LANES:(t + 1) * LANES] for t in range(n_tiles)]
    s = [jnp.where(valid, _dot_nt(qh[h], kt[h // 2]) + bias_ref[h], MASK_VALUE) for h in H]
    m = [jnp.max(s[h], axis=-1, keepdims=True) for h in H]
    e = [jnp.exp(s[h] - m[h]) for h in H]
    l = [jnp.sum(e[h], axis=-1, keepdims=True) for h in H]
    o = [_dot(e[h].astype(BF16), vt[h // 2]) / l[h] for h in H]
    o_ref[...] = jnp.concatenate(
        [jnp.where(head0, o[2 * t], o[2 * t + 1]) for t in range(n_tiles)], axis=1)


def _ca_attention(proj, rel_rows, B, S, tq):
    T = B * S
    nq = S // tq
    assert left_blocks(tq) == 2
    qcol = COL_CA // CA_WIDTH

    def kv_spec(col, back):
        return pl.BlockSpec((tq, CA_WIDTH), lambda b, i: (b * nq + jnp.maximum(i - back, 0), col))

    return pl.pallas_call(
        functools.partial(_ca_kernel, tq=tq),
        grid=(B, nq),
        in_specs=[
            pl.BlockSpec((tq, CA_WIDTH), lambda b, i: (b * nq + i, qcol)),
            kv_spec(qcol + 1, 2), kv_spec(qcol + 1, 1), kv_spec(qcol + 1, 0),
            kv_spec(qcol + 2, 2), kv_spec(qcol + 2, 1), kv_spec(qcol + 2, 0),
            pl.BlockSpec(rel_rows.shape, lambda b, i: (0, 0, 0)),
        ],
        out_specs=pl.BlockSpec((tq, CA_WIDTH), lambda b, i: (b * nq + i, 0)),
        out_shape=jax.ShapeDtypeStruct((T, CA_WIDTH), F32),
        scratch_shapes=[pltpu.VMEM((CA_WIDTH // HEAD_DIM, tq, 3 * tq), F32)],
        compiler_params=_cparams("parallel", "arbitrary"),
        name="ca_attention",
    )(proj, proj, proj, proj, proj, proj, proj, rel_rows)


def left_blocks(tq):
    return (CA_LEFT_CHUNKS * CHUNK) // tq


def _seg_sum(x, bd):
    parts = [_split_dot(x[:, c:c + LANES], bd) for c in range(0, x.shape[1], LANES)]
    return jnp.concatenate(parts, axis=1)


def _rwprep_kernel(*refs, tm, tiles_per_seq, first_layer):
    if first_layer:
        (main_ref, lora_ref, pmain_ref, plora_ref, mum_ref, mul_ref, w0_ref, w2_ref, a0_ref,
         a2_ref, g2_ref, kkw_ref, kaw_ref, bd_ref,
         r_out, lw_out, k_out, v_out, kk_out, a_out, g_out, vf_out) = refs
    else:
        (main_ref, lora_ref, pmain_ref, plora_ref, vf_ref, mum_ref, mul_ref, w0_ref, w2_ref,
         a0_ref, a2_ref, v0_ref, v2_ref, g2_ref, kkw_ref, kaw_ref, bd_ref,
         r_out, lw_out, k_out, v_out, kk_out, a_out, g_out) = refs
    i = pl.program_id(0)
    seq_start = (i % tiles_per_seq) == 0

    def shift_lerp(x, prev8, mu):
        prev = jnp.where(seq_start, 0.0, prev8[7:8, :])
        row = lax.broadcasted_iota(jnp.int32, x.shape, 0)
        xs = jnp.where(row == 0, prev, pltpu.roll(x, 1, 0))
        return x + (xs - x) * mu

    cm = shift_lerp(main_ref[...], pmain_ref[...], mum_ref[...])
    cl = shift_lerp(lora_ref[...], plora_ref[...], mul_ref[...])
    r = cm[:, :RW_WIDTH]
    k = cm[:, RW_WIDTH:2 * RW_WIDTH]
    v = cm[:, 2 * RW_WIDTH:]
    wa = cl[:, :LORA_WA]
    gv = cl[:, LORA_WA:]

    log_w = -_softplus(-(w0_ref[...] + _dot(jnp.tanh(wa).astype(BF16), w2_ref[...]))) - 0.5
    lw_out[...] = -jnp.exp(log_w)
    a = _sigmoid(a0_ref[...] + _dot(wa.astype(BF16), a2_ref[...]))
    g_out[...] = _dot(_sigmoid(gv).astype(BF16), g2_ref[...])
    if first_layer:
        vf_out[...] = v
    else:
        mix = _sigmoid(v0_ref[...] + _dot(gv.astype(BF16), v2_ref[...]))
        v = v + (vf_ref[...] - v) * mix
    kk = k * kkw_ref[...]
    norm = jnp.sqrt(_seg_sum(kk * kk, bd_ref[...]))
    kk_out[...] = kk / jnp.maximum(norm, 1e-12)
    k_out[...] = k * (1.0 + (a - 1.0) * kaw_ref[...])
    r_out[...] = r
    v_out[...] = v
    a_out[...] = a


def _rwprep(proj, v_first, prm, B, S, tm, first_layer):
    T = B * S
    nt = T // tm
    sub = tm // 8
    main_blk = COL_RW // (3 * RW_WIDTH)
    lora_blk = COL_LORA // (LORA_WA + LORA_GV)
    assert main_blk * 3 * RW_WIDTH == COL_RW and lora_blk * (LORA_WA + LORA_GV) == COL_LORA
    wl = LORA_WA + LORA_GV

    def full(a):
        return pl.BlockSpec(a.shape, lambda i: (0,) * a.ndim)

    row = pl.BlockSpec((tm, RW_WIDTH), lambda i: (i, 0))
    in_specs = [
        pl.BlockSpec((tm, 3 * RW_WIDTH), lambda i: (i, main_blk)),
        pl.BlockSpec((tm, wl), lambda i: (i, lora_blk)),
        pl.BlockSpec((8, 3 * RW_WIDTH), lambda i: (jnp.maximum(i * sub - 1, 0), main_blk)),
        pl.BlockSpec((8, wl), lambda i: (jnp.maximum(i * sub - 1, 0), lora_blk)),
    ]
    args = [proj, proj, proj, proj]
    if not first_layer:
        in_specs.append(row)
        args.append(v_first)
    names = ["mu_main", "mu_lora", "w0", "w2", "a0", "a2"]
    if not first_layer:
        names += ["v0", "v2"]
    names += ["g2", "k_k", "k_a", "bd"]
    for n in names:
        in_specs.append(full(prm[n]))
        args.append(prm[n])
    n_out = 8 if first_layer else 7
    outs = pl.pallas_call(
        functools.partial(_rwprep_kernel, tm=tm, tiles_per_seq=S // tm, first_layer=first_layer),
        grid=(nt,),
        in_specs=in_specs,
        out_specs=[row] * n_out,
        out_shape=[jax.ShapeDtypeStruct((T, RW_WIDTH), F32)] * n_out,
        compiler_params=_cparams("parallel"),
        name="rwkv_prep",
    )(*args)
    return outs


def _wkv_kernel(r_ref, lw_ref, k_ref, v_ref, kk_ref, a_ref, g_ref, rk_ref, lnw_ref, lnb_ref,
                tri_ref, bd_ref, o_ref, state_ref, y_ref, *, tb):
    C = CHUNK
    n_pairs = RW_WIDTH // LANES

    @pl.when(pl.program_id(1) == 0)
    def _():
        state_ref[...] = jnp.zeros_like(state_ref)

    lane = lax.broadcasted_iota(jnp.int32, (C, LANES), 1)
    head0 = lane < HEAD_DIM
    rows2 = lax.broadcasted_iota(jnp.int32, (2 * C, 2 * C), 0)
    cols2 = lax.broadcasted_iota(jnp.int32, (2 * C, 2 * C), 1)
    same = (rows2 // C) == (cols2 // C)
    strict = same & (cols2 < rows2)
    incl = same & (cols2 <= rows2)
    eye = (rows2 == cols2).astype(F32)
    blk_xor = rows2 ^ cols2
    tri = tri_ref[...]

    def stack(x):
        return jnp.concatenate([jnp.where(head0, x, 0.0), jnp.where(head0, 0.0, x)], axis=0)

    def prepare(g):
        units = [(dc, p) for dc in range(WKV_GROUP) for p in range(n_pairs)]
        P = range(len(units))

        def load(ref):
            return [ref[pl.ds(pl.multiple_of((g * WKV_GROUP + dc) * C, C), C),
                        p * LANES:(p + 1) * LANES] for dc, p in units]

        r, lw, k, v, kk, a = (load(x) for x in (r_ref, lw_ref, k_ref, v_ref, kk_ref, a_ref))
        cl = [_split_dot_left(tri, lw[p]) for p in P]
        e_pos = [jnp.exp(cl[p]) for p in P]
        e_neg = [jnp.exp(-cl[p]) for p in P]
        rt = [stack(r[p] * e_pos[p]).astype(BF16) for p in P]
        at = [stack(-kk[p] * jnp.exp(cl[p] - lw[p])).astype(BF16) for p in P]
        bt = [stack(kk[p] * a[p] * e_neg[p]).astype(BF16) for p in P]
        kt = [stack(k[p] * e_neg[p]).astype(BF16) for p in P]
        vs = [stack(v[p]).astype(BF16) for p in P]
        prod = [_dot_nt(jnp.concatenate([at[p], rt[p]], axis=0),
                        jnp.concatenate([bt[p], kt[p]], axis=0)) for p in P]
        l_ab = [jnp.where(strict, prod[p][:2 * C, :2 * C], 0.0) for p in P]
        l_ak = [jnp.where(strict, prod[p][:2 * C, 2 * C:], 0.0).astype(BF16) for p in P]
        q_rb = [jnp.where(incl, prod[p][2 * C:, :2 * C], 0.0).astype(BF16) for p in P]
        q_rk = [jnp.where(incl, prod[p][2 * C:, 2 * C:], 0.0).astype(BF16) for p in P]
        lakv = [_dot(l_ak[p], vs[p]).astype(BF16) for p in P]
        y0 = [_dot(q_rk[p], vs[p]) for p in P]
        kv = [_dot_tn(vs[p], kt[p]) for p in P]

        pm = [eye + jnp.where(blk_xor == 1, l_ab[p], 0.0) for p in P]
        s = 2
        while s < C:
            level = (blk_xor >= s) & (blk_xor < 2 * s)
            e = [jnp.where(level, l_ab[p], 0.0).astype(BF16) for p in P]
            pb = [pm[p].astype(BF16) for p in P]
            et = [_dot(e[p], pb[p]).astype(BF16) for p in P]
            pm = [pm[p] + _dot(pb[p], et[p]) for p in P]
            s *= 2
        tw = [_dot(pm[p].astype(BF16), jnp.concatenate([at[p], lakv[p]], axis=1)) for p in P]
        return tuple(
            (jnp.concatenate([tw[p][:, :LANES].astype(BF16), rt[p]], axis=0), tw[p][:, LANES:],
             q_rb[p], y0[p][:C] + y0[p][C:], bt[p], kv[p], e_pos[p][C - 1:C, :]) for p in P)

    def advance(g, prepared):
        P = range(n_pairs)
        states = [state_ref[p] for p in P]
        outs = []
        for dc in range(WKV_GROUP):
            w_r, u0, q_rb, y0, bt, kv, decay = zip(*prepared[dc * n_pairs:(dc + 1) * n_pairs])
            from_state = [_dot_nt(w_r[p], states[p].astype(BF16)) for p in P]
            ub = [(from_state[p][:2 * C] + u0[p]).astype(BF16) for p in P]
            states = [(states[p] + _dot_tn(ub[p], bt[p]) + kv[p]) * decay[p] for p in P]
            y2 = [from_state[p][2 * C:] + _dot(q_rb[p], ub[p]) for p in P]
            outs.append([y0[p] + y2[p][:C] + y2[p][C:] for p in P])
        for p in P:
            state_ref[p] = states[p]
        for dc in range(WKV_GROUP):
            rows = pl.ds(pl.multiple_of((g * WKV_GROUP + dc) * C, C), C)
            for p in P:
                y_ref[rows, p * LANES:(p + 1) * LANES] = outs[dc][p]

    n_groups = tb // (C * WKV_GROUP)

    def body(g, prepared):
        nxt = prepare(g)
        advance(g - 1, prepared)
        return nxt

    last = lax.fori_loop(1, n_groups, body, prepare(0))
    advance(n_groups - 1, last)

    bd = bd_ref[...]
    y = y_ref[...]
    mean = _seg_sum(y, bd) * (1.0 / HEAD_DIM)
    d = y - mean
    var = _seg_sum(d * d, bd) * (1.0 / HEAD_DIM)
    yn = d * lax.rsqrt(var + GN_EPS) * lnw_ref[...] + lnb_ref[...]
    bonus = _seg_sum(r_ref[...] * k_ref[...] * rk_ref[...], bd) * v_ref[...]
    o_ref[...] = (yn + bonus) * g_ref[...]


def _wkv(feats, prm, B, S, tb):
    T = B * S
    nt = S // tb
    r, lw, k, v, kk, a, g = feats
    row = pl.BlockSpec((tb, RW_WIDTH), lambda b, t: (b * nt + t, 0))

    def full(x):
        return pl.BlockSpec(x.shape, lambda b, t: (0,) * x.ndim)

    small = [prm["r_k"], prm["ln_w"], prm["ln_b"], prm["tri_chunk"], prm["bd"]]
    return pl.pallas_call(
        functools.partial(_wkv_kernel, tb=tb),
        grid=(B, nt),
        in_specs=[row] * 7 + [full(x) for x in small],
        out_specs=row,
        out_shape=jax.ShapeDtypeStruct((T, RW_WIDTH), F32),
        scratch_shapes=[pltpu.VMEM((RW_WIDTH // LANES, LANES, LANES), F32),
                        pltpu.VMEM((tb, RW_WIDTH), F32)],
        compiler_params=_cparams("parallel", "arbitrary"),
        name="wkv7",
    )(r, lw, k, v, kk, a, g, *small)


def _outffn_kernel(x_ref, sb_ref, ca_ref, rw_ref, sbg_ref, cag_ref, wo_ref, fg_ref, w1_ref, w2_ref,
                   fin_ref, o_ref, h_ref, acc_ref, *, final_norm):
    j = pl.program_id(1)

    @pl.when(j == 0)
    def _():
        sbn = _rms(sb_ref[...], sbg_ref[...]).astype(BF16)
        can = _rms(ca_ref[...], cag_ref[...]).astype(BF16)
        x = x_ref[...]
        x = x + _dot(sbn, wo_ref[:SB_WIDTH, :])
        x = x + _dot(can, wo_ref[SB_WIDTH:SB_WIDTH + CA_WIDTH, :])
        x = x + _dot(rw_ref[...].astype(BF16), wo_ref[SB_WIDTH + CA_WIDTH:, :])
        acc_ref[...] = x
        h_ref[...] = _rms(x, fg_ref[...]).astype(BF16)

    f = jnp.square(jnp.maximum(_dot(h_ref[...], w1_ref[...]), 0.0)).astype(BF16)
    acc_ref[...] += _dot(f, w2_ref[...])

    @pl.when(j == pl.num_programs(1) - 1)
    def _():
        y = acc_ref[...]
        if final_norm:
            y = _rms(y, fin_ref[...])
        o_ref[...] = y


def _outffn(x2d, sb, ca, rw, prm, tm, tf, final_norm):
    T, D = x2d.shape
    Fd = prm["w_ff_in"].shape[1]

    def rows(w):
        return pl.BlockSpec((tm, w), lambda i, j: (i, 0))

    def full(a):
        return pl.BlockSpec(a.shape, lambda i, j: (0,) * a.ndim)

    return pl.pallas_call(
        functools.partial(_outffn_kernel, final_norm=final_norm),
        grid=(T // tm, Fd // tf),
        in_specs=[
            rows(D), rows(SB_WIDTH), rows(CA_WIDTH), rows(RW_WIDTH),
            full(prm["sb_g"]), full(prm["ca_g"]), full(prm["w_out"]), full(prm["ffn_g"]),
            pl.BlockSpec((D, tf), lambda i, j: (0, j)),
            pl.BlockSpec((tf, D), lambda i, j: (j, 0)),
            full(prm["final_g"]),
        ],
        out_specs=rows(D),
        out_shape=jax.ShapeDtypeStruct((T, D), F32),
        scratch_shapes=[pltpu.VMEM((tm, D), BF16), pltpu.VMEM((tm, D), F32)],
        compiler_params=_cparams("parallel", "arbitrary"),
        name="outproj_ffn",
    )(x2d, sb, ca, rw, prm["sb_g"], prm["ca_g"], prm["w_out"], prm["ffn_g"],
      prm["w_ff_in"], prm["w_ff_out"], prm["final_g"])


def _layer_params(l, p, ca_tq):
    f32 = F32
    w_in = p["w_in"][l]
    D = w_in.shape[0]
    attn = w_in[:, :COL_RW]
    rest = w_in[:, COL_RW:]
    o_w = RW_WIDTH
    o_k = o_w + D_DECAY_LORA
    o_v = o_k + RW_WIDTH
    o_a = o_v + RW_WIDTH
    o_g = o_a + D_AAA_LORA
    r_c, w_c, k_c, v_c = rest[:, :o_w], rest[:, o_w:o_k], rest[:, o_k:o_v], rest[:, o_v:o_a]
    a_c, g_c = rest[:, o_a:o_g], rest[:, o_g:]
    pad_gv = LORA_GV - D_GATE_LORA - D_MV_LORA
    if l == 0:
        vd_c = jnp.zeros((D, D_MV_LORA), w_in.dtype)
    else:
        vd_c = p["w_vmix_down"][l - 1]
    w_proj = jnp.concatenate(
        [attn, r_c, k_c, v_c, w_c, a_c, g_c, vd_c, jnp.zeros((D, pad_gv), w_in.dtype)], axis=1)
    mu = p["rw_mu"][l]
    mu_main = jnp.concatenate([mu[:o_w], mu[o_k:o_v], mu[o_v:o_a]])[None, :]
    mu_lora = jnp.concatenate(
        [mu[o_w:o_k], mu[o_a:o_g], mu[o_g:], jnp.zeros((D_MV_LORA + pad_gv,), f32)])[None, :]

    def padrows(w, before, total):
        return jnp.pad(w, ((before, total - before - w.shape[0]), (0, 0))).astype(BF16)

    prm = {
        "norm_g": p["norm_mix_g"][l][None, :],
        "w_proj": w_proj.astype(BF16),
        "mu_main": mu_main,
        "mu_lora": mu_lora,
        "w0": p["rw_w0"][l][None, :],
        "w2": padrows(p["rw_w2"][l], 0, LORA_WA),
        "a0": p["rw_a0"][l][None, :],
        "a2": padrows(p["rw_a2"][l], D_DECAY_LORA, LORA_WA),
        "g2": padrows(p["rw_g2"][l], 0, LORA_GV),
        "k_k": p["rw_k_k"][l][None, :],
        "k_a": p["rw_k_a"][l][None, :],
        "r_k": p["rw_r_k"][l][None, :],
        "ln_w": p["rw_ln_w"][l][None, :],
        "ln_b": p["rw_ln_b"][l][None, :],
        "sb_g": p["sb_out_g"][l][None, :],
        "ca_g": p["ca_out_g"][l][None, :],
        "w_out": p["w_out"][l].astype(BF16),
        "ffn_g": p["norm_ffn_g"][l][None, :],
        "w_ff_in": p["w_ff_in"][l].astype(BF16),
        "w_ff_out": p["w_ff_out"][l].astype(BF16),
        "final_g": p["norm_final_g"][None, :],
    }
    if l > 0:
        prm["v0"] = p["rw_v0"][l - 1][None, :]
        prm["v2"] = padrows(p["rw_v2"][l - 1], D_GATE_LORA, LORA_GV)
    table = p["ca_rel_bias"][l]
    left = CA_LEFT_CHUNKS * CHUNK + ca_tq - REL_CLIP
    right = 4 * ca_tq - left - table.shape[1]
    rel = jnp.pad(table, ((0, 0), (left, right)), mode="edge")
    prm["rel_rows"] = rel.reshape(CA_WIDTH // HEAD_DIM, 1, 4 * ca_tq)
    return prm


def _constants(sb_tq):
    idx = jnp.arange(sb_tq)
    tri_sb = (idx[:, None] > idx[None, :]).astype(BF16)
    c = jnp.arange(CHUNK)
    tri_chunk = (c[:, None] >= c[None, :]).astype(BF16)
    ln = jnp.arange(LANES)
    bd = ((ln[:, None] // HEAD_DIM) == (ln[None, :] // HEAD_DIM)).astype(BF16)
    return tri_sb, tri_chunk, bd


def _forward(x, p, *, tm_proj, sb_tq, ca_tq, tm_prep, tb_wkv, tm_ffn, tf_ffn):
    B, S, D = x.shape
    depth = p["w_in"].shape[0]
    p = dict(p)
    for name in ("w_in", "w_vmix_down", "w_out", "w_ff_in", "w_ff_out"):
        p[name] = p[name].astype(BF16)
    tri_sb, tri_chunk, bd = _constants(sb_tq)
    x2d = x.reshape(B * S, D)
    v_first = None
    for l in range(depth):
        prm = _layer_params(l, p, ca_tq)
        prm["tri_chunk"] = tri_chunk
        prm["bd"] = bd
        proj = _inproj(x2d, prm["norm_g"], prm["w_proj"], tm_proj)
        sb = _sb_attention(proj, tri_sb, B, S, sb_tq)
        ca = _ca_attention(proj, prm["rel_rows"], B, S, ca_tq)
        feats = _rwprep(proj, v_first, prm, B, S, tm_prep, l == 0)
        if l == 0:
            v_first = feats[7]
        rw = _wkv(feats[:7], prm, B, S, tb_wkv)
        x2d = _outffn(x2d, sb, ca, rw, prm, tm_ffn, tf_ffn, l == depth - 1)
    return x2d.reshape(B, S, D)


def kernel(x, norm_mix_g, w_in, w_vmix_down, sb_out_g, ca_rel_bias, ca_out_g, rw_mu, rw_w0, rw_w2,
           rw_a0, rw_a2, rw_v0, rw_v2, rw_g2, rw_k_k, rw_k_a, rw_r_k, rw_ln_w, rw_ln_b, w_out,
           norm_ffn_g, w_ff_in, w_ff_out, norm_final_g):
    p = dict(norm_mix_g=norm_mix_g, w_in=w_in, w_vmix_down=w_vmix_down, sb_out_g=sb_out_g,
             ca_rel_bias=ca_rel_bias, ca_out_g=ca_out_g, rw_mu=rw_mu, rw_w0=rw_w0, rw_w2=rw_w2,
             rw_a0=rw_a0, rw_a2=rw_a2, rw_v0=rw_v0, rw_v2=rw_v2, rw_g2=rw_g2, rw_k_k=rw_k_k,
             rw_k_a=rw_k_a, rw_r_k=rw_r_k, rw_ln_w=rw_ln_w, rw_ln_b=rw_ln_b, w_out=w_out,
             norm_ffn_g=norm_ffn_g, w_ff_in=w_ff_in, w_ff_out=w_ff_out, norm_final_g=norm_final_g)
    return _forward(x, p, tm_proj=512, sb_tq=256, ca_tq=256, tm_prep=512, tb_wkv=1024,
                    tm_ffn=1024, tf_ffn=1024)
```

```python
import functools

import jax
import jax.numpy as jnp
from jax import lax
from jax.experimental import pallas as pl
from jax.experimental.pallas import tpu as pltpu

F32 = jnp.float32
BF16 = jnp.bfloat16

LANES = 128
HEAD_DIM = 64
CHUNK = 64
SB_WIDTH = 256
CA_WIDTH = 256
RW_WIDTH = 512
CA_LEFT_CHUNKS = 8
REL_CLIP = 256
D_DECAY_LORA = 64
D_AAA_LORA = 64
D_GATE_LORA = 160
D_MV_LORA = 32
RMS_EPS = 1e-5
GN_EPS = 64e-5
MASK_VALUE = -1e30
SB_DEAD_LOG = -104.0
WKV_GROUP = 4
PREP_ROW_CHUNKS = 4

COL_SB = 0
COL_CA = 3 * SB_WIDTH
COL_RW = COL_CA + 3 * CA_WIDTH
LORA_WA = 2 * HEAD_DIM
LORA_GV = 256
COL_LORA = COL_RW + 3 * RW_WIDTH
N_PROJ = COL_LORA + LORA_WA + LORA_GV

VMEM_LIMIT = 56 * 1024 * 1024


def _cparams(*sem):
    return pltpu.CompilerParams(dimension_semantics=sem, vmem_limit_bytes=VMEM_LIMIT)


def _dot(a, b):
    return jnp.dot(a, b, preferred_element_type=F32)


def _dot_nt(a, b):
    return lax.dot_general(a, b, (((1,), (1,)), ((), ())), preferred_element_type=F32)


def _dot_tn(a, b):
    return lax.dot_general(a, b, (((0,), (0,)), ((), ())), preferred_element_type=F32)


def _split_dot(x, m):
    hi = x.astype(BF16)
    lo = (x - hi.astype(F32)).astype(BF16)
    return _dot(hi, m) + _dot(lo, m)


def _split_dot_left(m, x):
    hi = x.astype(BF16)
    lo = (x - hi.astype(F32)).astype(BF16)
    return _dot(m, hi) + _dot(m, lo)


def _softplus(x):
    return jnp.maximum(x, 0.0) + jnp.log(1.0 + jnp.exp(-jnp.abs(x)))


def _sigmoid(x):
    return 1.0 / (1.0 + jnp.exp(-x))


def _rms(x, g):
    return x * lax.rsqrt(jnp.mean(x * x, axis=-1, keepdims=True) + RMS_EPS) * g


def _sb_kernel(q_ref, k_ref, v_ref, tri_ref, o_ref, acc_ref, car_ref, *, tq):
    i = pl.program_id(1)
    n_tiles = SB_WIDTH // LANES
    H = range(2 * n_tiles)
    lane = lax.broadcasted_iota(jnp.int32, (tq, LANES), 1)
    head0 = lane < HEAD_DIM
    q = q_ref[...] * (HEAD_DIM ** -0.5)
    qh = []
    for t in range(n_tiles):
        qt = q[:, t * LANES:(t + 1) * LANES]
        qh += [jnp.where(head0, qt, 0.0).astype(BF16), jnp.where(head0, 0.0, qt).astype(BF16)]
    tri = tri_ref[...]
    strict = (lax.broadcasted_iota(jnp.int32, (tq, tq), 1)
              < lax.broadcasted_iota(jnp.int32, (tq, tq), 0))
    acc_ref[...] = jnp.zeros_like(acc_ref)
    car_ref[...] = jnp.zeros_like(car_ref)

    def tile(kb, diag):
        start = pl.multiple_of(kb * tq, tq)
        kblk = k_ref[pl.ds(start, tq), :].astype(BF16)
        vblk = v_ref[pl.ds(start, tq), :].astype(BF16)
        kt = [kblk[:, t * LANES:(t + 1) * LANES] for t in range(n_tiles)]
        vt = [vblk[:, t * LANES:(t + 1) * LANES] for t in range(n_tiles)]
        z = [_dot_nt(qh[h], kt[h // 2]) for h in H]
        sp = [_softplus(z[h]) for h in H]
        log_1m = [-sp[h] for h in H]
        if diag:
            log_1m = [jnp.where(strict, log_1m[h], 0.0) for h in H]
        after = [_split_dot(log_1m[h], tri) for h in H]
        carry = [car_ref[h] for h in H]
        w = [jnp.exp((z[h] - sp[h]) + after[h] + carry[h]) for h in H]
        if diag:
            w = [jnp.where(strict, w[h], 0.0) for h in H]
        pv = [_dot(w[h].astype(BF16), vt[h // 2]) for h in H]
        for t in range(n_tiles):
            acc_ref[t] += jnp.where(head0, pv[2 * t], pv[2 * t + 1])
        for h in H:
            car_ref[h] = carry[h] + after[h][:, :1] + log_1m[h][:, :1]

    tile(i, True)

    def live():
        return jnp.max(car_ref[...]) > SB_DEAD_LOG

    def cond(s):
        return (s[0] < i) & s[1]

    def body(s):
        tile(i - 1 - s[0], False)
        return s[0] + 1, live()

    lax.while_loop(cond, body, (jnp.int32(0), live()))
    o_ref[...] = jnp.concatenate([acc_ref[t] for t in range(n_tiles)], axis=1)


def _sb_attention(proj, tri, B, S, tq):
    T = B * S
    nq = S // tq
    assert COL_SB == 0
    return pl.pallas_call(
        functools.partial(_sb_kernel, tq=tq),
        grid=(B, nq),
        in_specs=[
            pl.BlockSpec((tq, SB_WIDTH), lambda b, i: (b * nq + i, 0)),
            pl.BlockSpec((S, SB_WIDTH), lambda b, i: (b, 1)),
            pl.BlockSpec((S, SB_WIDTH), lambda b, i: (b, 2)),
            pl.BlockSpec((tq, tq), lambda b, i: (0, 0)),
        ],
        out_specs=pl.BlockSpec((tq, SB_WIDTH), lambda b, i: (b * nq + i, 0)),
        out_shape=jax.ShapeDtypeStruct((T, SB_WIDTH), F32),
        scratch_shapes=[pltpu.VMEM((SB_WIDTH // LANES, tq, LANES), F32),
                        pltpu.VMEM((SB_WIDTH // HEAD_DIM, tq, 1), F32)],
        compiler_params=_cparams("parallel", "arbitrary"),
        name="sb_attention",
    )(proj, proj, proj, tri)


def _ca_kernel(q_ref, k0_ref, k1_ref, k2_ref, v0_ref, v1_ref, v2_ref, rel_ref, o_ref, bias_ref,
               *, tq):
    i = pl.program_id(1)
    nk = 3 * tq
    left = CA_LEFT_CHUNKS * CHUNK
    n_tiles = CA_WIDTH // LANES
    H = range(2 * n_tiles)

    @pl.when(i == 0)
    def _():
        qi = lax.broadcasted_iota(jnp.int32, (tq, nk), 0)
        kj = lax.broadcasted_iota(jnp.int32, (tq, nk), 1)
        qc = qi // CHUNK
        kc = kj // CHUNK
        band = (kc >= qc) & (kc <= qc + CA_LEFT_CHUNKS)
        for h in H:
            row = jnp.broadcast_to(rel_ref[h], (tq, rel_ref.shape[-1]))
            toep = pltpu.roll(row, rel_ref.shape[-1] - tq, 1, stride=1, stride_axis=0)
            bias_ref[h] = jnp.where(band, toep[:, :nk], MASK_VALUE)

    lane = lax.broadcasted_iota(jnp.int32, (tq, LANES), 1)
    head0 = lane < HEAD_DIM
    q = q_ref[...] * (HEAD_DIM ** -0.5)
    k = jnp.concatenate([k0_ref[...], k1_ref[...], k2_ref[...]], axis=0).astype(BF16)
    v = jnp.concatenate([v0_ref[...], v1_ref[...], v2_ref[...]], axis=0).astype(BF16)
    key_abs = lax.broadcasted_iota(jnp.int32, (tq, nk), 1) + (i * tq - left)
    valid = key_abs >= 0
    qh = []
    for t in range(n_tiles):
        qt = q[:, t * LANES:(t + 1) * LANES]
        qh += [jnp.where(head0, qt, 0.0).astype(BF16), jnp.where(head0, 0.0, qt).astype(BF16)]
    kt = [k[:, t * LANES:(t + 1) * LANES] for t in range(n_tiles)]
    vt = [v[:, t * LANES:(t + 1) * LANES] for t in range(n_tiles)]
    s = [jnp.where(valid, _dot_nt(qh[h], kt[h // 2]) + bias_ref[h], MASK_VALUE) for h in H]
    m = [jnp.max(s[h], axis=-1, keepdims=True) for h in H]
    e = [jnp.exp(s[h] - m[h]) for h in H]
    l = [jnp.sum(e[h], axis=-1, keepdims=True) for h in H]
    o = [_dot(e[h].astype(BF16), vt[h // 2]) / l[h] for h in H]
    o_ref[...] = jnp.concatenate(
        [jnp.where(head0, o[2 * t], o[2 * t + 1]) for t in range(n_tiles)], axis=1)


def _ca_attention(proj, rel_rows, B, S, tq):
    T = B * S
    nq = S // tq
    assert left_blocks(tq) == 2
    qcol = COL_CA // CA_WIDTH

    def kv_spec(col, back):
        return pl.BlockSpec((tq, CA_WIDTH), lambda b, i: (b * nq + jnp.maximum(i - back, 0), col))

    return pl.pallas_call(
        functools.partial(_ca_kernel, tq=tq),
        grid=(B, nq),
        in_specs=[
            pl.BlockSpec((tq, CA_WIDTH), lambda b, i: (b * nq + i, qcol)),
            kv_spec(qcol + 1, 2), kv_spec(qcol + 1, 1), kv_spec(qcol + 1, 0),
            kv_spec(qcol + 2, 2), kv_spec(qcol + 2, 1), kv_spec(qcol + 2, 0),
            pl.BlockSpec(rel_rows.shape, lambda b, i: (0, 0, 0)),
        ],
        out_specs=pl.BlockSpec((tq, CA_WIDTH), lambda b, i: (b * nq + i, 0)),
        out_shape=jax.ShapeDtypeStruct((T, CA_WIDTH), F32),
        scratch_shapes=[pltpu.VMEM((CA_WIDTH // HEAD_DIM, tq, 3 * tq), F32)],
        compiler_params=_cparams("parallel", "arbitrary"),
        name="ca_attention",
    )(proj, proj, proj, proj, proj, proj, proj, rel_rows)


def left_blocks(tq):
    return (CA_LEFT_CHUNKS * CHUNK) // tq


def _seg_sum(x, bd):
    parts = [_split_dot(x[:, c:c + LANES], bd) for c in range(0, x.shape[1], LANES)]
    return jnp.concatenate(parts, axis=1)


def _inproj_prep_kernel(*refs, tm, tiles_per_seq, first_layer):
    if first_layer:
        (x_ref, ng_ref, w_ref, mu_ref, w0_ref, w2_ref, a0_ref, a2_ref, g2_ref, kkw_ref, kaw_ref,
         bd_ref, attn_out, r_out, lw_out, k_out, v_out, kk_out, a_out, g_out, vf_out,
         prev_ref) = refs
    else:
        (x_ref, ng_ref, w_ref, vf_ref, mu_ref, w0_ref, w2_ref, a0_ref, a2_ref, v0_ref, v2_ref,
         g2_ref, kkw_ref, kaw_ref, bd_ref, attn_out, r_out, lw_out, k_out, v_out, kk_out, a_out,
         g_out, prev_ref) = refs

    @pl.when(pl.program_id(0) % tiles_per_seq == 0)
    def _():
        prev_ref[...] = jnp.zeros_like(prev_ref)

    h = _rms(x_ref[...], ng_ref[...]).astype(BF16)
    rw = _dot(h, w_ref[:, COL_RW:])
    rc = tm // PREP_ROW_CHUNKS
    ac = COL_RW // PREP_ROW_CHUNKS
    mu = mu_ref[...]
    bd = bd_ref[...]
    prev = prev_ref[...]
    first_row = lax.broadcasted_iota(jnp.int32, (rc, rw.shape[1]), 0) == 0
    o_main = 3 * RW_WIDTH
    for c in range(PREP_ROW_CHUNKS):
        rows = slice(c * rc, (c + 1) * rc)
        x_c = rw[rows]
        shifted = jnp.where(first_row, prev, pltpu.roll(x_c, 1, 0))
        prev = x_c[rc - 1:rc, :]
        mixed = x_c + (shifted - x_c) * mu
        r = mixed[:, :RW_WIDTH]
        k = mixed[:, RW_WIDTH:2 * RW_WIDTH]
        v = mixed[:, 2 * RW_WIDTH:o_main]
        wa = mixed[:, o_main:o_main + LORA_WA]
        gv = mixed[:, o_main + LORA_WA:]

        log_w = -_softplus(-(w0_ref[...] + _dot(jnp.tanh(wa).astype(BF16), w2_ref[...]))) - 0.5
        lw_out[rows, :] = -jnp.exp(log_w)
        a = _sigmoid(a0_ref[...] + _dot(wa.astype(BF16), a2_ref[...]))
        g_out[rows, :] = _dot(_sigmoid(gv).astype(BF16), g2_ref[...])
        if first_layer:
            vf_out[rows, :] = v
        else:
            mix = _sigmoid(v0_ref[...] + _dot(gv.astype(BF16), v2_ref[...]))
            v = v + (vf_ref[rows, :] - v) * mix
        kk = k * kkw_ref[...]
        norm = jnp.sqrt(_seg_sum(kk * kk, bd))
        kk_out[rows, :] = kk / jnp.maximum(norm, 1e-12)
        k_out[rows, :] = k * (1.0 + (a - 1.0) * kaw_ref[...])
        r_out[rows, :] = r
        v_out[rows, :] = v
        a_out[rows, :] = a
        cols = slice(c * ac, (c + 1) * ac)
        attn_out[:, cols] = _dot(h, w_ref[:, cols])
    prev_ref[...] = prev


def _inproj_prep(x2d, v_first, prm, B, S, tm, first_layer):
    T, D = x2d.shape
    assert T == B * S and S % tm == 0

    def full(a):
        return pl.BlockSpec(a.shape, lambda i: (0,) * a.ndim)

    row = pl.BlockSpec((tm, RW_WIDTH), lambda i: (i, 0))
    in_specs = [pl.BlockSpec((tm, D), lambda i: (i, 0)), full(prm["norm_g"]),
                pl.BlockSpec(prm["w_proj"].shape, lambda i: (0, 0), pipeline_mode=pl.Buffered(1))]
    args = [x2d, prm["norm_g"], prm["w_proj"]]
    if not first_layer:
        in_specs.append(row)
        args.append(v_first)
    names = ["mu", "w0", "w2", "a0", "a2"]
    if not first_layer:
        names += ["v0", "v2"]
    names += ["g2", "k_k", "k_a", "bd"]
    for n in names:
        in_specs.append(full(prm[n]))
        args.append(prm[n])
    n_feat = 8 if first_layer else 7
    outs = pl.pallas_call(
        functools.partial(_inproj_prep_kernel, tm=tm, tiles_per_seq=S // tm,
                          first_layer=first_layer),
        grid=(T // tm,),
        in_specs=in_specs,
        out_specs=[pl.BlockSpec((tm, COL_RW), lambda i: (i, 0))] + [row] * n_feat,
        out_shape=[jax.ShapeDtypeStruct((T, COL_RW), F32)]
        + [jax.ShapeDtypeStruct((T, RW_WIDTH), F32)] * n_feat,
        scratch_shapes=[pltpu.VMEM((1, N_PROJ - COL_RW), F32)],
        compiler_params=_cparams("arbitrary"),
        name="inproj_prep",
    )(*args)
    return outs[0], outs[1:]


def _wkv_kernel(r_ref, lw_ref, k_ref, v_ref, kk_ref, a_ref, g_ref, rk_ref, lnw_ref, lnb_ref,
                tri_ref, bd_ref, o_ref, state_ref, y_ref, *, tb):
    C = CHUNK
    n_pairs = RW_WIDTH // LANES

    @pl.when(pl.program_id(1) == 0)
    def _():
        state_ref[...] = jnp.zeros_like(state_ref)

    lane = lax.broadcasted_iota(jnp.int32, (C, LANES), 1)
    head0 = lane < HEAD_DIM
    rows2 = lax.broadcasted_iota(jnp.int32, (2 * C, 2 * C), 0)
    cols2 = lax.broadcasted_iota(jnp.int32, (2 * C, 2 * C), 1)
    same = (rows2 // C) == (cols2 // C)
    strict = same & (cols2 < rows2)
    incl = same & (cols2 <= rows2)
    eye = (rows2 == cols2).astype(F32)
    blk_xor = rows2 ^ cols2
    tri = tri_ref[...]

    def stack(x):
        return jnp.concatenate([jnp.where(head0, x, 0.0), jnp.where(head0, 0.0, x)], axis=0)

    def prepare(g):
        units = [(dc, p) for dc in range(WKV_GROUP) for p in range(n_pairs)]
        P = range(len(units))

        def load(ref):
            return [ref[pl.ds(pl.multiple_of((g * WKV_GROUP + dc) * C, C), C),
                        p * LANES:(p + 1) * LANES] for dc, p in units]

        r, lw, k, v, kk, a = (load(x) for x in (r_ref, lw_ref, k_ref, v_ref, kk_ref, a_ref))
        cl = [_split_dot_left(tri, lw[p]) for p in P]
        e_pos = [jnp.exp(cl[p]) for p in P]
        e_neg = [jnp.exp(-cl[p]) for p in P]
        rt = [stack(r[p] * e_pos[p]).astype(BF16) for p in P]
        at = [stack(-kk[p] * jnp.exp(cl[p] - lw[p])).astype(BF16) for p in P]
        bt = [stack(kk[p] * a[p] * e_neg[p]).astype(BF16) for p in P]
        kt = [stack(k[p] * e_neg[p]).astype(BF16) for p in P]
        vs = [stack(v[p]).astype(BF16) for p in P]
        prod = [_dot_nt(jnp.concatenate([at[p], rt[p]], axis=0),
                        jnp.concatenate([bt[p], kt[p]], axis=0)) for p in P]
        l_ab = [jnp.where(strict, prod[p][:2 * C, :2 * C], 0.0) for p in P]
        l_ak = [jnp.where(strict, prod[p][:2 * C, 2 * C:], 0.0).astype(BF16) for p in P]
        q_rb = [jnp.where(incl, prod[p][2 * C:, :2 * C], 0.0).astype(BF16) for p in P]
        q_rk = [jnp.where(incl, prod[p][2 * C:, 2 * C:], 0.0).astype(BF16) for p in P]
        lakv = [_dot(l_ak[p], vs[p]).astype(BF16) for p in P]
        y0 = [_dot(q_rk[p], vs[p]) for p in P]
        kv = [_dot_tn(vs[p], kt[p]) for p in P]

        pm = [eye + jnp.where(blk_xor == 1, l_ab[p], 0.0) for p in P]
        s = 2
        while s < C:
            level = (blk_xor >= s) & (blk_xor < 2 * s)
            e = [jnp.where(level, l_ab[p], 0.0).astype(BF16) for p in P]
            pb = [pm[p].astype(BF16) for p in P]
            et = [_dot(e[p], pb[p]).astype(BF16) for p in P]
            pm = [pm[p] + _dot(pb[p], et[p]) for p in P]
            s *= 2
        tw = [_dot(pm[p].astype(BF16), jnp.concatenate([at[p], lakv[p]], axis=1)) for p in P]
        return tuple(
            (jnp.concatenate([tw[p][:, :LANES].astype(BF16), rt[p]], axis=0), tw[p][:, LANES:],
             q_rb[p], y0[p][:C] + y0[p][C:], bt[p], kv[p], e_pos[p][C - 1:C, :]) for p in P)

    def advance(g, prepared):
        P = range(n_pairs)
        states = [state_ref[p] for p in P]
        outs = []
        for dc in range(WKV_GROUP):
            w_r, u0, q_rb, y0, bt, kv, decay = zip(*prepared[dc * n_pairs:(dc + 1) * n_pairs])
            from_state = [_dot_nt(w_r[p], states[p].astype(BF16)) for p in P]
            ub = [(from_state[p][:2 * C] + u0[p]).astype(BF16) for p in P]
            states = [(states[p] + _dot_tn(ub[p], bt[p]) + kv[p]) * decay[p] for p in P]
            y2 = [from_state[p][2 * C:] + _dot(q_rb[p], ub[p]) for p in P]
            outs.append([y0[p] + y2[p][:C] + y2[p][C:] for p in P])
        for p in P:
            state_ref[p] = states[p]
        for dc in range(WKV_GROUP):
            rows = pl.ds(pl.multiple_of((g * WKV_GROUP + dc) * C, C), C)
            for p in P:
                y_ref[rows, p * LANES:(p + 1) * LANES] = outs[dc][p]

    n_groups = tb // (C * WKV_GROUP)

    def body(g, prepared):
        nxt = prepare(g)
        advance(g - 1, prepared)
        return nxt

    last = lax.fori_loop(1, n_groups, body, prepare(0))
    advance(n_groups - 1, last)

    bd = bd_ref[...]
    y = y_ref[...]
    mean = _seg_sum(y, bd) * (1.0 / HEAD_DIM)
    d = y - mean
    var = _seg_sum(d * d, bd) * (1.0 / HEAD_DIM)
    yn = d * lax.rsqrt(var + GN_EPS) * lnw_ref[...] + lnb_ref[...]
    bonus = _seg_sum(r_ref[...] * k_ref[...] * rk_ref[...], bd) * v_ref[...]
    o_ref[...] = (yn + bonus) * g_ref[...]


def _wkv(feats, prm, B, S, tb):
    T = B * S
    nt = S // tb
    r, lw, k, v, kk, a, g = feats
    row = pl.BlockSpec((tb, RW_WIDTH), lambda b, t: (b * nt + t, 0))

    def full(x):
        return pl.BlockSpec(x.shape, lambda b, t: (0,) * x.ndim)

    small = [prm["r_k"], prm["ln_w"], prm["ln_b"], prm["tri_chunk"], prm["bd"]]
    return pl.pallas_call(
        functools.partial(_wkv_kernel, tb=tb),
        grid=(B, nt),
        in_specs=[row] * 7 + [full(x) for x in small],
        out_specs=row,
        out_shape=jax.ShapeDtypeStruct((T, RW_WIDTH), F32),
        scratch_shapes=[pltpu.VMEM((RW_WIDTH // LANES, LANES, LANES), F32),
                        pltpu.VMEM((tb, RW_WIDTH), F32)],
        compiler_params=_cparams("parallel", "arbitrary"),
        name="wkv7",
    )(r, lw, k, v, kk, a, g, *small)


def _outffn_kernel(x_ref, sb_ref, ca_ref, rw_ref, sbg_ref, cag_ref, wo_ref, fg_ref, w1_ref, w2_ref,
                   fin_ref, o_ref, h_ref, acc_ref, *, final_norm):
    j = pl.program_id(1)

    @pl.when(j == 0)
    def _():
        sbn = _rms(sb_ref[...], sbg_ref[...]).astype(BF16)
        can = _rms(ca_ref[...], cag_ref[...]).astype(BF16)
        x = x_ref[...]
        x = x + _dot(sbn, wo_ref[:SB_WIDTH, :])
        x = x + _dot(can, wo_ref[SB_WIDTH:SB_WIDTH + CA_WIDTH, :])
        x = x + _dot(rw_ref[...].astype(BF16), wo_ref[SB_WIDTH + CA_WIDTH:, :])
        acc_ref[...] = x
        h_ref[...] = _rms(x, fg_ref[...]).astype(BF16)

    f = jnp.square(jnp.maximum(_dot(h_ref[...], w1_ref[...]), 0.0)).astype(BF16)
    acc_ref[...] += _dot(f, w2_ref[...])

    @pl.when(j == pl.num_programs(1) - 1)
    def _():
        y = acc_ref[...]
        if final_norm:
            y = _rms(y, fin_ref[...])
        o_ref[...] = y


def _outffn(x2d, sb, ca, rw, prm, tm, tf, final_norm):
    T, D = x2d.shape
    Fd = prm["w_ff_in"].shape[1]

    def rows(w):
        return pl.BlockSpec((tm, w), lambda i, j: (i, 0))

    def full(a):
        return pl.BlockSpec(a.shape, lambda i, j: (0,) * a.ndim)

    return pl.pallas_call(
        functools.partial(_outffn_kernel, final_norm=final_norm),
        grid=(T // tm, Fd // tf),
        in_specs=[
            rows(D), rows(SB_WIDTH), rows(CA_WIDTH), rows(RW_WIDTH),
            full(prm["sb_g"]), full(prm["ca_g"]), full(prm["w_out"]), full(prm["ffn_g"]),
            pl.BlockSpec((D, tf), lambda i, j: (0, j)),
            pl.BlockSpec((tf, D), lambda i, j: (j, 0)),
            full(prm["final_g"]),
        ],
        out_specs=rows(D),
        out_shape=jax.ShapeDtypeStruct((T, D), F32),
        scratch_shapes=[pltpu.VMEM((tm, D), BF16), pltpu.VMEM((tm, D), F32)],
        compiler_params=_cparams("parallel", "arbitrary"),
        name="outproj_ffn",
    )(x2d, sb, ca, rw, prm["sb_g"], prm["ca_g"], prm["w_out"], prm["ffn_g"],
      prm["w_ff_in"], prm["w_ff_out"], prm["final_g"])


def _layer_params(l, p, ca_tq):
    w_in = p["w_in"][l]
    D = w_in.shape[0]
    attn = w_in[:, :COL_RW]
    rest = w_in[:, COL_RW:]
    o_w = RW_WIDTH
    o_k = o_w + D_DECAY_LORA
    o_v = o_k + RW_WIDTH
    o_a = o_v + RW_WIDTH
    o_g = o_a + D_AAA_LORA
    r_c, w_c, k_c, v_c = rest[:, :o_w], rest[:, o_w:o_k], rest[:, o_k:o_v], rest[:, o_v:o_a]
    a_c, g_c = rest[:, o_a:o_g], rest[:, o_g:]
    pad_gv = LORA_GV - D_GATE_LORA - D_MV_LORA
    if l == 0:
        vd_c = jnp.zeros((D, D_MV_LORA), w_in.dtype)
    else:
        vd_c = p["w_vmix_down"][l - 1]
    w_proj = jnp.concatenate(
        [attn, r_c, k_c, v_c, w_c, a_c, g_c, vd_c, jnp.zeros((D, pad_gv), w_in.dtype)], axis=1)
    mu = p["rw_mu"][l]
    mu_main = jnp.concatenate([mu[:o_w], mu[o_k:o_v], mu[o_v:o_a]])[None, :]
    mu_lora = jnp.concatenate(
        [mu[o_w:o_k], mu[o_a:o_g], mu[o_g:], jnp.zeros((D_MV_LORA + pad_gv,), F32)])[None, :]

    def padrows(w, before, total):
        return jnp.pad(w, ((before, total - before - w.shape[0]), (0, 0))).astype(BF16)

    prm = {
        "norm_g": p["norm_mix_g"][l][None, :],
        "w_proj": w_proj,
        "mu": jnp.concatenate([mu_main, mu_lora], axis=1),
        "w0": p["rw_w0"][l][None, :],
        "w2": padrows(p["rw_w2"][l], 0, LORA_WA),
        "a0": p["rw_a0"][l][None, :],
        "a2": padrows(p["rw_a2"][l], D_DECAY_LORA, LORA_WA),
        "g2": padrows(p["rw_g2"][l], 0, LORA_GV),
        "k_k": p["rw_k_k"][l][None, :],
        "k_a": p["rw_k_a"][l][None, :],
        "r_k": p["rw_r_k"][l][None, :],
        "ln_w": p["rw_ln_w"][l][None, :],
        "ln_b": p["rw_ln_b"][l][None, :],
        "sb_g": p["sb_out_g"][l][None, :],
        "ca_g": p["ca_out_g"][l][None, :],
        "w_out": p["w_out"][l],
        "ffn_g": p["norm_ffn_g"][l][None, :],
        "w_ff_in": p["w_ff_in"][l],
        "w_ff_out": p["w_ff_out"][l],
        "final_g": p["norm_final_g"][None, :],
    }
    if l > 0:
        prm["v0"] = p["rw_v0"][l - 1][None, :]
        prm["v2"] = padrows(p["rw_v2"][l - 1], D_GATE_LORA, LORA_GV)
    table = p["ca_rel_bias"][l]
    left = CA_LEFT_CHUNKS * CHUNK + ca_tq - REL_CLIP
    right = 4 * ca_tq - left - table.shape[1]
    rel = jnp.pad(table, ((0, 0), (left, right)), mode="edge")
    prm["rel_rows"] = rel.reshape(CA_WIDTH // HEAD_DIM, 1, 4 * ca_tq)
    return prm


def _constants(sb_tq):
    idx = jnp.arange(sb_tq)
    tri_sb = (idx[:, None] > idx[None, :]).astype(BF16)
    c = jnp.arange(CHUNK)
    tri_chunk = (c[:, None] >= c[None, :]).astype(BF16)
    ln = jnp.arange(LANES)
    bd = ((ln[:, None] // HEAD_DIM) == (ln[None, :] // HEAD_DIM)).astype(BF16)
    return tri_sb, tri_chunk, bd


def _forward(x, p, *, tm_proj, sb_tq, ca_tq, tb_wkv, tm_ffn, tf_ffn):
    B, S, D = x.shape
    depth = p["w_in"].shape[0]
    p = dict(p)
    for name in ("w_in", "w_vmix_down", "w_out", "w_ff_in", "w_ff_out"):
        p[name] = p[name].astype(BF16)
    tri_sb, tri_chunk, bd = _constants(sb_tq)
    x2d = x.reshape(B * S, D)
    v_first = None
    for l in range(depth):
        prm = _layer_params(l, p, ca_tq)
        prm["tri_chunk"] = tri_chunk
        prm["bd"] = bd
        proj, feats = _inproj_prep(x2d, v_first, prm, B, S, tm_proj, l == 0)
        sb = _sb_attention(proj, tri_sb, B, S, sb_tq)
        ca = _ca_attention(proj, prm["rel_rows"], B, S, ca_tq)
        if l == 0:
            v_first = feats[7]
        rw = _wkv(feats[:7], prm, B, S, tb_wkv)
        x2d = _outffn(x2d, sb, ca, rw, prm, tm_ffn, tf_ffn, l == depth - 1)
    return x2d.reshape(B, S, D)


def kernel(x, norm_mix_g, w_in, w_vmix_down, sb_out_g, ca_rel_bias, ca_out_g, rw_mu, rw_w0, rw_w2,
           rw_a0, rw_a2, rw_v0, rw_v2, rw_g2, rw_k_k, rw_k_a, rw_r_k, rw_ln_w, rw_ln_b, w_out,
           norm_ffn_g, w_ff_in, w_ff_out, norm_final_g):
    p = dict(norm_mix_g=norm_mix_g, w_in=w_in, w_vmix_down=w_vmix_down, sb_out_g=sb_out_g,
             ca_rel_bias=ca_rel_bias, ca_out_g=ca_out_g, rw_mu=rw_mu, rw_w0=rw_w0, rw_w2=rw_w2,
             rw_a0=rw_a0, rw_a2=rw_a2, rw_v0=rw_v0, rw_v2=rw_v2, rw_g2=rw_g2, rw_k_k=rw_k_k,
             rw_k_a=rw_k_a, rw_r_k=rw_r_k, rw_ln_w=rw_ln_w, rw_ln_b=rw_ln_b, w_out=w_out,
             norm_ffn_g=norm_ffn_g, w_ff_in=w_ff_in, w_ff_out=w_ff_out, norm_final_g=norm_final_g)
    return _forward(x, p, tm_proj=512, sb_tq=256, ca_tq=256, tb_wkv=1024,
                    tm_ffn=1024, tf_ffn=1024)
```

```python
import functools

import jax
import jax.numpy as jnp
from jax import lax
from jax.experimental import pallas as pl
from jax.experimental.pallas import tpu as pltpu

F32 = jnp.float32
BF16 = jnp.bfloat16

LANES = 128
HEAD_DIM = 64
CHUNK = 64
SB_WIDTH = 256
CA_WIDTH = 256
RW_WIDTH = 512
CA_LEFT_CHUNKS = 8
REL_CLIP = 256
D_DECAY_LORA = 64
D_AAA_LORA = 64
D_GATE_LORA = 160
D_MV_LORA = 32
RMS_EPS = 1e-5
GN_EPS = 64e-5
MASK_VALUE = -1e30
SB_DEAD_LOG = -104.0
WKV_GROUP = 4
PREP_ROW_CHUNKS = 4

COL_SB = 0
COL_CA = 3 * SB_WIDTH
COL_RW = COL_CA + 3 * CA_WIDTH
LORA_WA = 2 * HEAD_DIM
LORA_GV = 256
COL_LORA = COL_RW + 3 * RW_WIDTH
N_PROJ = COL_LORA + LORA_WA + LORA_GV

VMEM_LIMIT = 56 * 1024 * 1024


def _cparams(*sem):
    return pltpu.CompilerParams(dimension_semantics=sem, vmem_limit_bytes=VMEM_LIMIT)


def _dot(a, b):
    return jnp.dot(a, b, preferred_element_type=F32)


def _dot_nt(a, b):
    return lax.dot_general(a, b, (((1,), (1,)), ((), ())), preferred_element_type=F32)


def _dot_tn(a, b):
    return lax.dot_general(a, b, (((0,), (0,)), ((), ())), preferred_element_type=F32)


def _split_dot(x, m):
    hi = x.astype(BF16)
    lo = (x - hi.astype(F32)).astype(BF16)
    return _dot(hi, m) + _dot(lo, m)


def _split_dot_left(m, x):
    hi = x.astype(BF16)
    lo = (x - hi.astype(F32)).astype(BF16)
    return _dot(m, hi) + _dot(m, lo)


def _softplus(x):
    return jnp.maximum(x, 0.0) + jnp.log(1.0 + jnp.exp(-jnp.abs(x)))


def _sigmoid(x):
    return 1.0 / (1.0 + jnp.exp(-x))


def _rms(x, g):
    return x * lax.rsqrt(jnp.mean(x * x, axis=-1, keepdims=True) + RMS_EPS) * g


def _sb_kernel(q_ref, k_ref, v_ref, tri_ref, o_ref, acc_ref, car_ref, *, tq):
    i = pl.program_id(1)
    n_tiles = SB_WIDTH // LANES
    H = range(2 * n_tiles)
    lane = lax.broadcasted_iota(jnp.int32, (tq, LANES), 1)
    head0 = lane < HEAD_DIM
    q = q_ref[...] * (HEAD_DIM ** -0.5)
    qh = []
    for t in range(n_tiles):
        qt = q[:, t * LANES:(t + 1) * LANES]
        qh += [jnp.where(head0, qt, 0.0).astype(BF16), jnp.where(head0, 0.0, qt).astype(BF16)]
    tri = tri_ref[...]
    strict = (lax.broadcasted_iota(jnp.int32, (tq, tq), 1)
              < lax.broadcasted_iota(jnp.int32, (tq, tq), 0))
    acc_ref[...] = jnp.zeros_like(acc_ref)
    car_ref[...] = jnp.zeros_like(car_ref)

    def tiles(specs):
        n = len(specs)
        units = [(ti, h) for ti in range(n) for h in H]
        U = range(len(units))
        kt, vt = [], []
        for kb, _ in specs:
            start = pl.multiple_of(kb * tq, tq)
            kblk = k_ref[pl.ds(start, tq), :].astype(BF16)
            vblk = v_ref[pl.ds(start, tq), :].astype(BF16)
            kt.append([kblk[:, t * LANES:(t + 1) * LANES] for t in range(n_tiles)])
            vt.append([vblk[:, t * LANES:(t + 1) * LANES] for t in range(n_tiles)])
        diag = [specs[ti][1] for ti, _ in units]
        z = [_dot_nt(qh[h], kt[ti][h // 2]) for ti, h in units]
        sp = [_softplus(z[u]) for u in U]
        log_1m = [jnp.where(strict, -sp[u], 0.0) if diag[u] else -sp[u] for u in U]
        after = [_split_dot(log_1m[u], tri) for u in U]
        total = [after[u][:, :1] + log_1m[u][:, :1] for u in U]
        carry = [car_ref[h] for h in H]
        for ti in range(1, n):
            carry += [carry[(ti - 1) * len(H) + h] + total[(ti - 1) * len(H) + h] for h in H]
        w = [jnp.exp((z[u] - sp[u]) + after[u] + carry[u]) for u in U]
        w = [jnp.where(strict, w[u], 0.0) if diag[u] else w[u] for u in U]
        pv = [_dot(w[u].astype(BF16), vt[ti][h // 2]) for u, (ti, h) in enumerate(units)]
        for t in range(n_tiles):
            upd = [jnp.where(head0, pv[ti * len(H) + 2 * t], pv[ti * len(H) + 2 * t + 1])
                   for ti in range(n)]
            acc_ref[t] += functools.reduce(lambda x, y: x + y, upd)
        for h in H:
            last = (n - 1) * len(H) + h
            car_ref[h] = carry[last] + total[last]

    @pl.when(i == 0)
    def _():
        tiles([(i, True)])

    @pl.when(i > 0)
    def _():
        tiles([(i, True), (i - 1, False)])

    def live():
        return jnp.max(car_ref[...]) > SB_DEAD_LOG

    def cond(s):
        return (s[0] < i) & s[1]

    def body(s):
        tiles([(i - 1 - s[0], False)])
        return s[0] + 1, live()

    lax.while_loop(cond, body, (jnp.int32(1), live()))
    o_ref[...] = jnp.concatenate([acc_ref[t] for t in range(n_tiles)], axis=1)


def _sb_attention(proj, tri, B, S, tq):
    T = B * S
    nq = S // tq
    assert COL_SB == 0
    return pl.pallas_call(
        functools.partial(_sb_kernel, tq=tq),
        grid=(B, nq),
        in_specs=[
            pl.BlockSpec((tq, SB_WIDTH), lambda b, i: (b * nq + i, 0)),
            pl.BlockSpec((S, SB_WIDTH), lambda b, i: (b, 1)),
            pl.BlockSpec((S, SB_WIDTH), lambda b, i: (b, 2)),
            pl.BlockSpec((tq, tq), lambda b, i: (0, 0)),
        ],
        out_specs=pl.BlockSpec((tq, SB_WIDTH), lambda b, i: (b * nq + i, 0)),
        out_shape=jax.ShapeDtypeStruct((T, SB_WIDTH), F32),
        scratch_shapes=[pltpu.VMEM((SB_WIDTH // LANES, tq, LANES), F32),
                        pltpu.VMEM((SB_WIDTH // HEAD_DIM, tq, 1), F32)],
        compiler_params=_cparams("parallel", "arbitrary"),
        name="sb_attention",
    )(proj, proj, proj, tri)


def _ca_kernel(q_ref, k0_ref, k1_ref, k2_ref, v0_ref, v1_ref, v2_ref, rel_ref, o_ref, bias_ref,
               *, tq):
    i = pl.program_id(1)
    nk = 3 * tq
    left = CA_LEFT_CHUNKS * CHUNK
    n_tiles = CA_WIDTH // LANES
    H = range(2 * n_tiles)

    @pl.when(i == 0)
    def _():
        qi = lax.broadcasted_iota(jnp.int32, (tq, nk), 0)
        kj = lax.broadcasted_iota(jnp.int32, (tq, nk), 1)
        qc = qi // CHUNK
        kc = kj // CHUNK
        band = (kc >= qc) & (kc <= qc + CA_LEFT_CHUNKS)
        for h in H:
            row = jnp.broadcast_to(rel_ref[h], (tq, rel_ref.shape[-1]))
            toep = pltpu.roll(row, rel_ref.shape[-1] - tq, 1, stride=1, stride_axis=0)
            bias_ref[h] = jnp.where(band, toep[:, :nk], MASK_VALUE)

    lane = lax.broadcasted_iota(jnp.int32, (tq, LANES), 1)
    head0 = lane < HEAD_DIM
    q = q_ref[...] * (HEAD_DIM ** -0.5)
    k = jnp.concatenate([k0_ref[...], k1_ref[...], k2_ref[...]], axis=0).astype(BF16)
    v = jnp.concatenate([v0_ref[...], v1_ref[...], v2_ref[...]], axis=0).astype(BF16)
    key_abs = lax.broadcasted_iota(jnp.int32, (tq, nk), 1) + (i * tq - left)
    valid = key_abs >= 0
    qh = []
    for t in range(n_tiles):
        qt = q[:, t * LANES:(t + 1) * LANES]
        qh += [jnp.where(head0, qt, 0.0).astype(BF16), jnp.where(head0, 0.0, qt).astype(BF16)]
    kt = [k[:, t * LANES:(t + 1) * LANES] for t in range(n_tiles)]
    vt = [v[:, t * LANES:(t + 1) * LANES] for t in range(n_tiles)]
    s = [jnp.where(valid, _dot_nt(qh[h], kt[h // 2]) + bias_ref[h], MASK_VALUE) for h in H]
    m = [jnp.max(s[h], axis=-1, keepdims=True) for h in H]
    e = [jnp.exp(s[h] - m[h]) for h in H]
    l = [jnp.sum(e[h], axis=-1, keepdims=True) for h in H]
    o = [_dot(e[h].astype(BF16), vt[h // 2]) / l[h] for h in H]
    o_ref[...] = jnp.concatenate(
        [jnp.where(head0, o[2 * t], o[2 * t + 1]) for t in range(n_tiles)], axis=1)


def _ca_attention(proj, rel_rows, B, S, tq):
    T = B * S
    nq = S // tq
    assert left_blocks(tq) == 2
    qcol = COL_CA // CA_WIDTH

    def kv_spec(col, back):
        return pl.BlockSpec((tq, CA_WIDTH), lambda b, i: (b * nq + jnp.maximum(i - back, 0), col))

    return pl.pallas_call(
        functools.partial(_ca_kernel, tq=tq),
        grid=(B, nq),
        in_specs=[
            pl.BlockSpec((tq, CA_WIDTH), lambda b, i: (b * nq + i, qcol)),
            kv_spec(qcol + 1, 2), kv_spec(qcol + 1, 1), kv_spec(qcol + 1, 0),
            kv_spec(qcol + 2, 2), kv_spec(qcol + 2, 1), kv_spec(qcol + 2, 0),
            pl.BlockSpec(rel_rows.shape, lambda b, i: (0, 0, 0)),
        ],
        out_specs=pl.BlockSpec((tq, CA_WIDTH), lambda b, i: (b * nq + i, 0)),
        out_shape=jax.ShapeDtypeStruct((T, CA_WIDTH), F32),
        scratch_shapes=[pltpu.VMEM((CA_WIDTH // HEAD_DIM, tq, 3 * tq), F32)],
        compiler_params=_cparams("parallel", "arbitrary"),
        name="ca_attention",
    )(proj, proj, proj, proj, proj, proj, proj, rel_rows)


def left_blocks(tq):
    return (CA_LEFT_CHUNKS * CHUNK) // tq


def _seg_sum(x, bd, split=True):
    if split:
        parts = [_split_dot(x[:, c:c + LANES], bd) for c in range(0, x.shape[1], LANES)]
    else:
        parts = [_dot(x[:, c:c + LANES].astype(BF16), bd) for c in range(0, x.shape[1], LANES)]
    return jnp.concatenate(parts, axis=1)


def _inproj_prep_kernel(*refs, tm, tiles_per_seq, first_layer):
    if first_layer:
        (x_ref, ng_ref, w_ref, mu_ref, w0_ref, w2_ref, a0_ref, a2_ref, g2_ref, kkw_ref, kaw_ref,
         bd_ref, attn_out, r_out, lw_out, k_out, v_out, kk_out, a_out, g_out, vf_out,
         prev_ref) = refs
    else:
        (x_ref, ng_ref, w_ref, vf_ref, mu_ref, w0_ref, w2_ref, a0_ref, a2_ref, v0_ref, v2_ref,
         g2_ref, kkw_ref, kaw_ref, bd_ref, attn_out, r_out, lw_out, k_out, v_out, kk_out, a_out,
         g_out, prev_ref) = refs

    @pl.when(pl.program_id(0) % tiles_per_seq == 0)
    def _():
        prev_ref[...] = jnp.zeros_like(prev_ref)

    h = _rms(x_ref[...], ng_ref[...]).astype(BF16)
    rw = _dot(h, w_ref[:, COL_RW:])
    rc = tm // PREP_ROW_CHUNKS
    ac = COL_RW // PREP_ROW_CHUNKS
    mu = mu_ref[...]
    bd = bd_ref[...]
    prev = prev_ref[...]
    first_row = lax.broadcasted_iota(jnp.int32, (rc, rw.shape[1]), 0) == 0
    o_main = 3 * RW_WIDTH
    for c in range(PREP_ROW_CHUNKS):
        rows = slice(c * rc, (c + 1) * rc)
        x_c = rw[rows]
        shifted = jnp.where(first_row, prev, pltpu.roll(x_c, 1, 0))
        prev = x_c[rc - 1:rc, :]
        mixed = x_c + (shifted - x_c) * mu
        r = mixed[:, :RW_WIDTH]
        k = mixed[:, RW_WIDTH:2 * RW_WIDTH]
        v = mixed[:, 2 * RW_WIDTH:o_main]
        wa = mixed[:, o_main:o_main + LORA_WA]
        gv = mixed[:, o_main + LORA_WA:]

        log_w = -_softplus(-(w0_ref[...] + _dot(jnp.tanh(wa).astype(BF16), w2_ref[...]))) - 0.5
        lw_out[rows, :] = -jnp.exp(log_w)
        a = _sigmoid(a0_ref[...] + _dot(wa.astype(BF16), a2_ref[...]))
        g_out[rows, :] = _dot(_sigmoid(gv).astype(BF16), g2_ref[...])
        if first_layer:
            vf_out[rows, :] = v
        else:
            mix = _sigmoid(v0_ref[...] + _dot(gv.astype(BF16), v2_ref[...]))
            v = v + (vf_ref[rows, :] - v) * mix
        kk = k * kkw_ref[...]
        norm = jnp.sqrt(_seg_sum(kk * kk, bd))
        kk_out[rows, :] = kk / jnp.maximum(norm, 1e-12)
        k_out[rows, :] = k * (1.0 + (a - 1.0) * kaw_ref[...])
        r_out[rows, :] = r
        v_out[rows, :] = v
        a_out[rows, :] = a
        cols = slice(c * ac, (c + 1) * ac)
        attn_out[:, cols] = _dot(h, w_ref[:, cols])
    prev_ref[...] = prev


def _inproj_prep(x2d, v_first, prm, B, S, tm, first_layer):
    T, D = x2d.shape
    assert T == B * S and S % tm == 0

    def full(a):
        return pl.BlockSpec(a.shape, lambda i: (0,) * a.ndim)

    row = pl.BlockSpec((tm, RW_WIDTH), lambda i: (i, 0))
    in_specs = [pl.BlockSpec((tm, D), lambda i: (i, 0)), full(prm["norm_g"]),
                pl.BlockSpec(prm["w_proj"].shape, lambda i: (0, 0), pipeline_mode=pl.Buffered(1))]
    args = [x2d, prm["norm_g"], prm["w_proj"]]
    if not first_layer:
        in_specs.append(row)
        args.append(v_first)
    names = ["mu", "w0", "w2", "a0", "a2"]
    if not first_layer:
        names += ["v0", "v2"]
    names += ["g2", "k_k", "k_a", "bd"]
    for n in names:
        in_specs.append(full(prm[n]))
        args.append(prm[n])
    n_feat = 8 if first_layer else 7
    outs = pl.pallas_call(
        functools.partial(_inproj_prep_kernel, tm=tm, tiles_per_seq=S // tm,
                          first_layer=first_layer),
        grid=(T // tm,),
        in_specs=in_specs,
        out_specs=[pl.BlockSpec((tm, COL_RW), lambda i: (i, 0))] + [row] * n_feat,
        out_shape=[jax.ShapeDtypeStruct((T, COL_RW), F32)]
        + [jax.ShapeDtypeStruct((T, RW_WIDTH), F32)] * n_feat,
        scratch_shapes=[pltpu.VMEM((1, N_PROJ - COL_RW), F32)],
        compiler_params=_cparams("arbitrary"),
        name="inproj_prep",
    )(*args)
    return outs[0], outs[1:]


def _wkv_kernel(r_ref, lw_ref, k_ref, v_ref, kk_ref, a_ref, g_ref, rk_ref, lnw_ref, lnb_ref,
                tri_ref, bd_ref, o_ref, state_ref, *, tb):
    C = CHUNK
    n_pairs = RW_WIDTH // LANES

    @pl.when(pl.program_id(1) == 0)
    def _():
        state_ref[...] = jnp.zeros_like(state_ref)

    lane = lax.broadcasted_iota(jnp.int32, (C, LANES), 1)
    head0 = lane < HEAD_DIM
    rows2 = lax.broadcasted_iota(jnp.int32, (2 * C, 2 * C), 0)
    cols2 = lax.broadcasted_iota(jnp.int32, (2 * C, 2 * C), 1)
    same = (rows2 // C) == (cols2 // C)
    strict = same & (cols2 < rows2)
    incl = same & (cols2 <= rows2)
    eye = (rows2 == cols2).astype(F32)
    blk_xor = rows2 ^ cols2
    tri = tri_ref[...]
    bd = bd_ref[...]

    def stack(x):
        return jnp.concatenate([jnp.where(head0, x, 0.0), jnp.where(head0, 0.0, x)], axis=0)

    def prepare(g, steps=()):
        units = [(dc, p) for dc in range(WKV_GROUP) for p in range(n_pairs)]
        P = range(len(units))
        steps = list(steps)

        def issue_step():
            if steps:
                steps.pop(0)()

        def load(ref):
            return [ref[pl.ds(pl.multiple_of((g * WKV_GROUP + dc) * C, C), C),
                        p * LANES:(p + 1) * LANES] for dc, p in units]

        r, lw, k, v, kk, a = (load(x) for x in (r_ref, lw_ref, k_ref, v_ref, kk_ref, a_ref))
        cl = [_split_dot_left(tri, lw[p]) for p in P]
        e_pos = [jnp.exp(cl[p]) for p in P]
        e_neg = [jnp.exp(-cl[p]) for p in P]
        rt = [stack(r[p] * e_pos[p]).astype(BF16) for p in P]
        at = [stack(-kk[p] * jnp.exp(cl[p] - lw[p])).astype(BF16) for p in P]
        bt = [stack(kk[p] * a[p] * e_neg[p]).astype(BF16) for p in P]
        kt = [stack(k[p] * e_neg[p]).astype(BF16) for p in P]
        vs = [stack(v[p]).astype(BF16) for p in P]
        prod = [_dot_nt(jnp.concatenate([at[p], rt[p]], axis=0),
                        jnp.concatenate([bt[p], kt[p]], axis=0)) for p in P]
        l_ab = [jnp.where(strict, prod[p][:2 * C, :2 * C], 0.0) for p in P]
        l_ak = [jnp.where(strict, prod[p][:2 * C, 2 * C:], 0.0).astype(BF16) for p in P]
        q_rb = [jnp.where(incl, prod[p][2 * C:, :2 * C], 0.0).astype(BF16) for p in P]
        q_rk = [jnp.where(incl, prod[p][2 * C:, 2 * C:], 0.0).astype(BF16) for p in P]
        issue_step()
        lakv = [_dot(l_ak[p], vs[p]).astype(BF16) for p in P]
        y0 = [_dot(q_rk[p], vs[p]) for p in P]
        kv = [_dot_tn(vs[p], kt[p]) for p in P]

        pm = [eye + jnp.where(blk_xor == 1, l_ab[p], 0.0) for p in P]
        s = 2
        while s < C:
            level = (blk_xor >= s) & (blk_xor < 2 * s)
            e = [jnp.where(level, l_ab[p], 0.0).astype(BF16) for p in P]
            pb = [pm[p].astype(BF16) for p in P]
            et = [_dot(e[p], pb[p]).astype(BF16) for p in P]
            pm = [pm[p] + _dot(pb[p], et[p]) for p in P]
            s *= 2
            if s in (8, 32):
                issue_step()
        tw = [_dot(pm[p].astype(BF16), jnp.concatenate([at[p], lakv[p]], axis=1)) for p in P]
        while steps:
            issue_step()
        return tuple(
            (jnp.concatenate([tw[p][:, :LANES].astype(BF16), rt[p]], axis=0), tw[p][:, LANES:],
             q_rb[p], y0[p][:C] + y0[p][C:], bt[p], kv[p], e_pos[p][C - 1:C, :]) for p in P)

    def advance(g, prepared):
        P = range(n_pairs)
        cur = {"states": [state_ref[p] for p in P], "outs": []}

        def chunk_step(dc):
            def run():
                w_r, u0, q_rb, y0, bt, kv, decay = zip(*prepared[dc * n_pairs:(dc + 1) * n_pairs])
                states = cur["states"]
                from_state = [_dot_nt(w_r[p], states[p].astype(BF16)) for p in P]
                ub = [(from_state[p][:2 * C] + u0[p]).astype(BF16) for p in P]
                cur["states"] = [(states[p] + _dot_tn(ub[p], bt[p]) + kv[p]) * decay[p] for p in P]
                y2 = [from_state[p][2 * C:] + _dot(q_rb[p], ub[p]) for p in P]
                cur["outs"].append([y0[p] + y2[p][:C] + y2[p][C:] for p in P])
            return run

        def finish():
            for p in P:
                state_ref[p] = cur["states"][p]
            rows = pl.ds(pl.multiple_of(g * (WKV_GROUP * C), WKV_GROUP * C), WKV_GROUP * C)
            y = jnp.concatenate([jnp.concatenate(o, axis=1) for o in cur["outs"]], axis=0)
            mean = _seg_sum(y, bd) * (1.0 / HEAD_DIM)
            d = y - mean
            var = _seg_sum(d * d, bd, split=False) * (1.0 / HEAD_DIM)
            yn = d * lax.rsqrt(var + GN_EPS) * lnw_ref[...] + lnb_ref[...]
            bonus = _seg_sum(r_ref[rows, :] * k_ref[rows, :] * rk_ref[...], bd,
                             split=False) * v_ref[rows, :]
            o_ref[rows, :] = (yn + bonus) * g_ref[rows, :]

        return [chunk_step(dc) for dc in range(WKV_GROUP)] + [finish]

    n_groups = tb // (C * WKV_GROUP)

    def body(g, prepared):
        return prepare(g, advance(g - 1, prepared))

    last = lax.fori_loop(1, n_groups, body, prepare(0))
    for step in advance(n_groups - 1, last):
        step()


def _wkv(feats, prm, B, S, tb):
    T = B * S
    nt = S // tb
    r, lw, k, v, kk, a, g = feats
    row = pl.BlockSpec((tb, RW_WIDTH), lambda b, t: (b * nt + t, 0))

    def full(x):
        return pl.BlockSpec(x.shape, lambda b, t: (0,) * x.ndim)

    small = [prm["r_k"], prm["ln_w"], prm["ln_b"], prm["tri_chunk"], prm["bd"]]
    return pl.pallas_call(
        functools.partial(_wkv_kernel, tb=tb),
        grid=(B, nt),
        in_specs=[row] * 7 + [full(x) for x in small],
        out_specs=row,
        out_shape=jax.ShapeDtypeStruct((T, RW_WIDTH), F32),
        scratch_shapes=[pltpu.VMEM((RW_WIDTH // LANES, LANES, LANES), F32)],
        compiler_params=_cparams("parallel", "arbitrary"),
        name="wkv7",
    )(r, lw, k, v, kk, a, g, *small)


def _outffn_kernel(x_ref, sb_ref, ca_ref, rw_ref, sbg_ref, cag_ref, wo_ref, fg_ref, w1_ref, w2_ref,
                   fin_ref, o_ref, h_ref, acc_ref, *, final_norm):
    j = pl.program_id(1)

    @pl.when(j == 0)
    def _():
        sbn = _rms(sb_ref[...], sbg_ref[...]).astype(BF16)
        can = _rms(ca_ref[...], cag_ref[...]).astype(BF16)
        x = x_ref[...]
        x = x + _dot(sbn, wo_ref[:SB_WIDTH, :])
        x = x + _dot(can, wo_ref[SB_WIDTH:SB_WIDTH + CA_WIDTH, :])
        x = x + _dot(rw_ref[...].astype(BF16), wo_ref[SB_WIDTH + CA_WIDTH:, :])
        acc_ref[...] = x
        h_ref[...] = _rms(x, fg_ref[...]).astype(BF16)

    f = jnp.square(jnp.maximum(_dot(h_ref[...], w1_ref[...]), 0.0)).astype(BF16)
    acc_ref[...] += _dot(f, w2_ref[...])

    @pl.when(j == pl.num_programs(1) - 1)
    def _():
        y = acc_ref[...]
        if final_norm:
            y = _rms(y, fin_ref[...])
        o_ref[...] = y


def _outffn(x2d, sb, ca, rw, prm, tm, tf, final_norm):
    T, D = x2d.shape
    Fd = prm["w_ff_in"].shape[1]

    def rows(w):
        return pl.BlockSpec((tm, w), lambda i, j: (i, 0))

    def full(a):
        return pl.BlockSpec(a.shape, lambda i, j: (0,) * a.ndim)

    return pl.pallas_call(
        functools.partial(_outffn_kernel, final_norm=final_norm),
        grid=(T // tm, Fd // tf),
        in_specs=[
            rows(D), rows(SB_WIDTH), rows(CA_WIDTH), rows(RW_WIDTH),
            full(prm["sb_g"]), full(prm["ca_g"]), full(prm["w_out"]), full(prm["ffn_g"]),
            pl.BlockSpec((D, tf), lambda i, j: (0, j)),
            pl.BlockSpec((tf, D), lambda i, j: (j, 0)),
            full(prm["final_g"]),
        ],
        out_specs=rows(D),
        out_shape=jax.ShapeDtypeStruct((T, D), F32),
        scratch_shapes=[pltpu.VMEM((tm, D), BF16), pltpu.VMEM((tm, D), F32)],
        compiler_params=_cparams("parallel", "arbitrary"),
        name="outproj_ffn",
    )(x2d, sb, ca, rw, prm["sb_g"], prm["ca_g"], prm["w_out"], prm["ffn_g"],
      prm["w_ff_in"], prm["w_ff_out"], prm["final_g"])


def _layer_params(l, p, ca_tq):
    w_in = p["w_in"][l]
    D = w_in.shape[0]
    attn = w_in[:, :COL_RW]
    rest = w_in[:, COL_RW:]
    o_w = RW_WIDTH
    o_k = o_w + D_DECAY_LORA
    o_v = o_k + RW_WIDTH
    o_a = o_v + RW_WIDTH
    o_g = o_a + D_AAA_LORA
    r_c, w_c, k_c, v_c = rest[:, :o_w], rest[:, o_w:o_k], rest[:, o_k:o_v], rest[:, o_v:o_a]
    a_c, g_c = rest[:, o_a:o_g], rest[:, o_g:]
    pad_gv = LORA_GV - D_GATE_LORA - D_MV_LORA
    if l == 0:
        vd_c = jnp.zeros((D, D_MV_LORA), w_in.dtype)
    else:
        vd_c = p["w_vmix_down"][l - 1]
    w_proj = jnp.concatenate(
        [attn, r_c, k_c, v_c, w_c, a_c, g_c, vd_c, jnp.zeros((D, pad_gv), w_in.dtype)], axis=1)
    mu = p["rw_mu"][l]
    mu_main = jnp.concatenate([mu[:o_w], mu[o_k:o_v], mu[o_v:o_a]])[None, :]
    mu_lora = jnp.concatenate(
        [mu[o_w:o_k], mu[o_a:o_g], mu[o_g:], jnp.zeros((D_MV_LORA + pad_gv,), F32)])[None, :]

    def padrows(w, before, total):
        return jnp.pad(w, ((before, total - before - w.shape[0]), (0, 0))).astype(BF16)

    prm = {
        "norm_g": p["norm_mix_g"][l][None, :],
        "w_proj": w_proj,
        "mu": jnp.concatenate([mu_main, mu_lora], axis=1),
        "w0": p["rw_w0"][l][None, :],
        "w2": padrows(p["rw_w2"][l], 0, LORA_WA),
        "a0": p["rw_a0"][l][None, :],
        "a2": padrows(p["rw_a2"][l], D_DECAY_LORA, LORA_WA),
        "g2": padrows(p["rw_g2"][l], 0, LORA_GV),
        "k_k": p["rw_k_k"][l][None, :],
        "k_a": p["rw_k_a"][l][None, :],
        "r_k": p["rw_r_k"][l][None, :],
        "ln_w": p["rw_ln_w"][l][None, :],
        "ln_b": p["rw_ln_b"][l][None, :],
        "sb_g": p["sb_out_g"][l][None, :],
        "ca_g": p["ca_out_g"][l][None, :],
        "w_out": p["w_out"][l],
        "ffn_g": p["norm_ffn_g"][l][None, :],
        "w_ff_in": p["w_ff_in"][l],
        "w_ff_out": p["w_ff_out"][l],
        "final_g": p["norm_final_g"][None, :],
    }
    if l > 0:
        prm["v0"] = p["rw_v0"][l - 1][None, :]
        prm["v2"] = padrows(p["rw_v2"][l - 1], D_GATE_LORA, LORA_GV)
    table = p["ca_rel_bias"][l]
    left = CA_LEFT_CHUNKS * CHUNK + ca_tq - REL_CLIP
    right = 4 * ca_tq - left - table.shape[1]
    rel = jnp.pad(table, ((0, 0), (left, right)), mode="edge")
    prm["rel_rows"] = rel.reshape(CA_WIDTH // HEAD_DIM, 1, 4 * ca_tq)
    return prm


def _constants(sb_tq):
    idx = jnp.arange(sb_tq)
    tri_sb = (idx[:, None] > idx[None, :]).astype(BF16)
    c = jnp.arange(CHUNK)
    tri_chunk = (c[:, None] >= c[None, :]).astype(BF16)
    ln = jnp.arange(LANES)
    bd = ((ln[:, None] // HEAD_DIM) == (ln[None, :] // HEAD_DIM)).astype(BF16)
    return tri_sb, tri_chunk, bd


def _forward(x, p, *, tm_proj, sb_tq, ca_tq, tb_wkv, tm_ffn, tf_ffn):
    B, S, D = x.shape
    depth = p["w_in"].shape[0]
    p = dict(p)
    for name in ("w_in", "w_vmix_down", "w_out", "w_ff_in", "w_ff_out"):
        p[name] = p[name].astype(BF16)
    tri_sb, tri_chunk, bd = _constants(sb_tq)
    x2d = x.reshape(B * S, D)
    v_first = None
    for l in range(depth):
        prm = _layer_params(l, p, ca_tq)
        prm["tri_chunk"] = tri_chunk
        prm["bd"] = bd
        proj, feats = _inproj_prep(x2d, v_first, prm, B, S, tm_proj, l == 0)
        sb = _sb_attention(proj, tri_sb, B, S, sb_tq)
        ca = _ca_attention(proj, prm["rel_rows"], B, S, ca_tq)
        if l == 0:
            v_first = feats[7]
        rw = _wkv(feats[:7], prm, B, S, tb_wkv)
        x2d = _outffn(x2d, sb, ca, rw, prm, tm_ffn, tf_ffn, l == depth - 1)
    return x2d.reshape(B, S, D)


def kernel(x, norm_mix_g, w_in, w_vmix_down, sb_out_g, ca_rel_bias, ca_out_g, rw_mu, rw_w0, rw_w2,
           rw_a0, rw_a2, rw_v0, rw_v2, rw_g2, rw_k_k, rw_k_a, rw_r_k, rw_ln_w, rw_ln_b, w_out,
           norm_ffn_g, w_ff_in, w_ff_out, norm_final_g):
    p = dict(norm_mix_g=norm_mix_g, w_in=w_in, w_vmix_down=w_vmix_down, sb_out_g=sb_out_g,
             ca_rel_bias=ca_rel_bias, ca_out_g=ca_out_g, rw_mu=rw_mu, rw_w0=rw_w0, rw_w2=rw_w2,
             rw_a0=rw_a0, rw_a2=rw_a2, rw_v0=rw_v0, rw_v2=rw_v2, rw_g2=rw_g2, rw_k_k=rw_k_k,
             rw_k_a=rw_k_a, rw_r_k=rw_r_k, rw_ln_w=rw_ln_w, rw_ln_b=rw_ln_b, w_out=w_out,
             norm_ffn_g=norm_ffn_g, w_ff_in=w_ff_in, w_ff_out=w_ff_out, norm_final_g=norm_final_g)
    return _forward(x, p, tm_proj=512, sb_tq=256, ca_tq=256, tb_wkv=1024,
                    tm_ffn=1024, tf_ffn=1024)
```

```python
import functools

import jax
import jax.numpy as jnp
from jax import lax
from jax.experimental import pallas as pl
from jax.experimental.pallas import tpu as pltpu

F32 = jnp.float32
BF16 = jnp.bfloat16

LANES = 128
HEAD_DIM = 64
CHUNK = 64
SB_WIDTH = 256
CA_WIDTH = 256
RW_WIDTH = 512
CA_LEFT_CHUNKS = 8
REL_CLIP = 256
D_DECAY_LORA = 64
D_AAA_LORA = 64
D_GATE_LORA = 160
D_MV_LORA = 32
RMS_EPS = 1e-5
GN_EPS = 64e-5
MASK_VALUE = -1e30
SB_DEAD_LOG = -104.0
WKV_GROUP = 4
PREP_ROW_CHUNKS = 4
FEATURE_DTYPES = (BF16, F32, BF16, BF16, BF16, BF16, BF16, F32)

COL_SB = 0
COL_CA = 3 * SB_WIDTH
COL_RW = COL_CA + 3 * CA_WIDTH
LORA_WA = 2 * HEAD_DIM
LORA_GV = 256
COL_LORA = COL_RW + 3 * RW_WIDTH
N_PROJ = COL_LORA + LORA_WA + LORA_GV

VMEM_LIMIT = 56 * 1024 * 1024


def _cparams(*sem):
    return pltpu.CompilerParams(dimension_semantics=sem, vmem_limit_bytes=VMEM_LIMIT)


def _dot(a, b):
    return jnp.dot(a, b, preferred_element_type=F32)


def _dot_nt(a, b):
    return lax.dot_general(a, b, (((1,), (1,)), ((), ())), preferred_element_type=F32)


def _dot_tn(a, b):
    return lax.dot_general(a, b, (((0,), (0,)), ((), ())), preferred_element_type=F32)


def _split_dot(x, m):
    hi = x.astype(BF16)
    lo = (x - hi.astype(F32)).astype(BF16)
    return _dot(hi, m) + _dot(lo, m)


def _split_dot_left(m, x):
    hi = x.astype(BF16)
    lo = (x - hi.astype(F32)).astype(BF16)
    return _dot(m, hi) + _dot(m, lo)


def _softplus(x):
    return jnp.maximum(x, 0.0) + jnp.log(1.0 + jnp.exp(-jnp.abs(x)))


def _sigmoid(x):
    return 1.0 / (1.0 + jnp.exp(-x))


def _rms(x, g):
    return x * lax.rsqrt(jnp.mean(x * x, axis=-1, keepdims=True) + RMS_EPS) * g


def _sb_kernel(q_ref, k_ref, v_ref, tri_ref, o_ref, acc_ref, car_ref, *, tq):
    i = pl.program_id(1)
    n_tiles = SB_WIDTH // LANES
    H = range(2 * n_tiles)
    lane = lax.broadcasted_iota(jnp.int32, (tq, LANES), 1)
    head0 = lane < HEAD_DIM
    q = q_ref[...] * (HEAD_DIM ** -0.5)
    qh = []
    for t in range(n_tiles):
        qt = q[:, t * LANES:(t + 1) * LANES]
        qh += [jnp.where(head0, qt, 0.0).astype(BF16), jnp.where(head0, 0.0, qt).astype(BF16)]
    tri = tri_ref[...]
    strict = (lax.broadcasted_iota(jnp.int32, (tq, tq), 1)
              < lax.broadcasted_iota(jnp.int32, (tq, tq), 0))
    acc_ref[...] = jnp.zeros_like(acc_ref)
    car_ref[...] = jnp.zeros_like(car_ref)

    def tiles(specs):
        n = len(specs)
        units = [(ti, h) for ti in range(n) for h in H]
        U = range(len(units))
        kt, vt = [], []
        for kb, _ in specs:
            start = pl.multiple_of(kb * tq, tq)
            kblk = k_ref[pl.ds(start, tq), :].astype(BF16)
            vblk = v_ref[pl.ds(start, tq), :].astype(BF16)
            kt.append([kblk[:, t * LANES:(t + 1) * LANES] for t in range(n_tiles)])
            vt.append([vblk[:, t * LANES:(t + 1) * LANES] for t in range(n_tiles)])
        diag = [specs[ti][1] for ti, _ in units]
        z = [_dot_nt(qh[h], kt[ti][h // 2]) for ti, h in units]
        sp = [_softplus(z[u]) for u in U]
        log_1m = [jnp.where(strict, -sp[u], 0.0) if diag[u] else -sp[u] for u in U]
        after = [_split_dot(log_1m[u], tri) for u in U]
        total = [after[u][:, :1] + log_1m[u][:, :1] for u in U]
        carry = [car_ref[h] for h in H]
        for ti in range(1, n):
            carry += [carry[(ti - 1) * len(H) + h] + total[(ti - 1) * len(H) + h] for h in H]
        w = [jnp.exp((z[u] - sp[u]) + after[u] + carry[u]) for u in U]
        w = [jnp.where(strict, w[u], 0.0) if diag[u] else w[u] for u in U]
        pv = [_dot(w[u].astype(BF16), vt[ti][h // 2]) for u, (ti, h) in enumerate(units)]
        for t in range(n_tiles):
            upd = [jnp.where(head0, pv[ti * len(H) + 2 * t], pv[ti * len(H) + 2 * t + 1])
                   for ti in range(n)]
            acc_ref[t] += functools.reduce(lambda x, y: x + y, upd)
        for h in H:
            last = (n - 1) * len(H) + h
            car_ref[h] = carry[last] + total[last]

    @pl.when(i == 0)
    def _():
        tiles([(i, True)])

    @pl.when(i > 0)
    def _():
        tiles([(i, True), (i - 1, False)])

    def live():
        return jnp.max(car_ref[...]) > SB_DEAD_LOG

    def cond(s):
        return (s[0] < i) & s[1]

    def body(s):
        tiles([(i - 1 - s[0], False)])
        return s[0] + 1, live()

    lax.while_loop(cond, body, (jnp.int32(1), live()))
    o_ref[...] = jnp.concatenate([acc_ref[t] for t in range(n_tiles)], axis=1)


def _sb_attention(proj, tri, B, S, tq):
    T = B * S
    nq = S // tq
    assert COL_SB == 0
    return pl.pallas_call(
        functools.partial(_sb_kernel, tq=tq),
        grid=(B, nq),
        in_specs=[
            pl.BlockSpec((tq, SB_WIDTH), lambda b, i: (b * nq + i, 0)),
            pl.BlockSpec((S, SB_WIDTH), lambda b, i: (b, 1)),
            pl.BlockSpec((S, SB_WIDTH), lambda b, i: (b, 2)),
            pl.BlockSpec((tq, tq), lambda b, i: (0, 0)),
        ],
        out_specs=pl.BlockSpec((tq, SB_WIDTH), lambda b, i: (b * nq + i, 0)),
        out_shape=jax.ShapeDtypeStruct((T, SB_WIDTH), F32),
        scratch_shapes=[pltpu.VMEM((SB_WIDTH // LANES, tq, LANES), F32),
                        pltpu.VMEM((SB_WIDTH // HEAD_DIM, tq, 1), F32)],
        compiler_params=_cparams("parallel", "arbitrary"),
        name="sb_attention",
    )(proj, proj, proj, tri)


def _ca_kernel(q_ref, k0_ref, k1_ref, k2_ref, v0_ref, v1_ref, v2_ref, rel_ref, o_ref, bias_ref,
               *, tq):
    i = pl.program_id(1)
    nk = 3 * tq
    left = CA_LEFT_CHUNKS * CHUNK
    n_tiles = CA_WIDTH // LANES
    H = range(2 * n_tiles)

    @pl.when(i == 0)
    def _():
        qi = lax.broadcasted_iota(jnp.int32, (tq, nk), 0)
        kj = lax.broadcasted_iota(jnp.int32, (tq, nk), 1)
        qc = qi // CHUNK
        kc = kj // CHUNK
        band = (kc >= qc) & (kc <= qc + CA_LEFT_CHUNKS)
        for h in H:
            row = jnp.broadcast_to(rel_ref[h], (tq, rel_ref.shape[-1]))
            toep = pltpu.roll(row, rel_ref.shape[-1] - tq, 1, stride=1, stride_axis=0)
            bias_ref[h] = jnp.where(band, toep[:, :nk], MASK_VALUE)

    lane = lax.broadcasted_iota(jnp.int32, (tq, LANES), 1)
    head0 = lane < HEAD_DIM
    q = q_ref[...] * (HEAD_DIM ** -0.5)
    k = jnp.concatenate([k0_ref[...], k1_ref[...], k2_ref[...]], axis=0).astype(BF16)
    v = jnp.concatenate([v0_ref[...], v1_ref[...], v2_ref[...]], axis=0).astype(BF16)
    key_abs = lax.broadcasted_iota(jnp.int32, (tq, nk), 1) + (i * tq - left)
    valid = key_abs >= 0
    qh = []
    for t in range(n_tiles):
        qt = q[:, t * LANES:(t + 1) * LANES]
        qh += [jnp.where(head0, qt, 0.0).astype(BF16), jnp.where(head0, 0.0, qt).astype(BF16)]
    kt = [k[:, t * LANES:(t + 1) * LANES] for t in range(n_tiles)]
    vt = [v[:, t * LANES:(t + 1) * LANES] for t in range(n_tiles)]
    s = [jnp.where(valid, _dot_nt(qh[h], kt[h // 2]) + bias_ref[h], MASK_VALUE) for h in H]
    m = [jnp.max(s[h], axis=-1, keepdims=True) for h in H]
    e = [jnp.exp(s[h] - m[h]) for h in H]
    l = [jnp.sum(e[h], axis=-1, keepdims=True) for h in H]
    o = [_dot(e[h].astype(BF16), vt[h // 2]) / l[h] for h in H]
    o_ref[...] = jnp.concatenate(
        [jnp.where(head0, o[2 * t], o[2 * t + 1]) for t in range(n_tiles)], axis=1)


def _ca_attention(proj, rel_rows, B, S, tq):
    T = B * S
    nq = S // tq
    assert left_blocks(tq) == 2
    qcol = COL_CA // CA_WIDTH

    def kv_spec(col, back):
        return pl.BlockSpec((tq, CA_WIDTH), lambda b, i: (b * nq + jnp.maximum(i - back, 0), col))

    return pl.pallas_call(
        functools.partial(_ca_kernel, tq=tq),
        grid=(B, nq),
        in_specs=[
            pl.BlockSpec((tq, CA_WIDTH), lambda b, i: (b * nq + i, qcol)),
            kv_spec(qcol + 1, 2), kv_spec(qcol + 1, 1), kv_spec(qcol + 1, 0),
            kv_spec(qcol + 2, 2), kv_spec(qcol + 2, 1), kv_spec(qcol + 2, 0),
            pl.BlockSpec(rel_rows.shape, lambda b, i: (0, 0, 0)),
        ],
        out_specs=pl.BlockSpec((tq, CA_WIDTH), lambda b, i: (b * nq + i, 0)),
        out_shape=jax.ShapeDtypeStruct((T, CA_WIDTH), F32),
        scratch_shapes=[pltpu.VMEM((CA_WIDTH // HEAD_DIM, tq, 3 * tq), F32)],
        compiler_params=_cparams("parallel", "arbitrary"),
        name="ca_attention",
    )(proj, proj, proj, proj, proj, proj, proj, rel_rows)


def left_blocks(tq):
    return (CA_LEFT_CHUNKS * CHUNK) // tq


def _seg_sum(x, bd, split=True):
    if split:
        parts = [_split_dot(x[:, c:c + LANES], bd) for c in range(0, x.shape[1], LANES)]
    else:
        parts = [_dot(x[:, c:c + LANES].astype(BF16), bd) for c in range(0, x.shape[1], LANES)]
    return jnp.concatenate(parts, axis=1)


def _inproj_prep_kernel(*refs, tm, tiles_per_seq, first_layer):
    if first_layer:
        (x_ref, ng_ref, w_ref, mu_ref, w0_ref, w2_ref, a0_ref, a2_ref, g2_ref, kkw_ref, kaw_ref,
         bd_ref, attn_out, r_out, lw_out, k_out, v_out, kk_out, a_out, g_out, vf_out,
         prev_ref) = refs
    else:
        (x_ref, ng_ref, w_ref, vf_ref, mu_ref, w0_ref, w2_ref, a0_ref, a2_ref, v0_ref, v2_ref,
         g2_ref, kkw_ref, kaw_ref, bd_ref, attn_out, r_out, lw_out, k_out, v_out, kk_out, a_out,
         g_out, prev_ref) = refs

    @pl.when(pl.program_id(0) % tiles_per_seq == 0)
    def _():
        prev_ref[...] = jnp.zeros_like(prev_ref)

    h = _rms(x_ref[...], ng_ref[...]).astype(BF16)
    rw = _dot(h, w_ref[:, COL_RW:])
    rc = tm // PREP_ROW_CHUNKS
    ac = COL_RW // PREP_ROW_CHUNKS
    mu = mu_ref[...]
    bd = bd_ref[...]
    prev = prev_ref[...]
    first_row = lax.broadcasted_iota(jnp.int32, (rc, rw.shape[1]), 0) == 0
    o_main = 3 * RW_WIDTH
    for c in range(PREP_ROW_CHUNKS):
        rows = slice(c * rc, (c + 1) * rc)
        x_c = rw[rows]
        shifted = jnp.where(first_row, prev, pltpu.roll(x_c, 1, 0))
        prev = x_c[rc - 1:rc, :]
        mixed = x_c + (shifted - x_c) * mu
        r = mixed[:, :RW_WIDTH]
        k = mixed[:, RW_WIDTH:2 * RW_WIDTH]
        v = mixed[:, 2 * RW_WIDTH:o_main]
        wa = mixed[:, o_main:o_main + LORA_WA]
        gv = mixed[:, o_main + LORA_WA:]

        log_w = -_softplus(-(w0_ref[...] + _dot(jnp.tanh(wa).astype(BF16), w2_ref[...]))) - 0.5
        lw_out[rows, :] = -jnp.exp(log_w)
        a = _sigmoid(a0_ref[...] + _dot(wa.astype(BF16), a2_ref[...]))
        g_out[rows, :] = _dot(_sigmoid(gv).astype(BF16), g2_ref[...]).astype(BF16)
        if first_layer:
            vf_out[rows, :] = v
        else:
            mix = _sigmoid(v0_ref[...] + _dot(gv.astype(BF16), v2_ref[...]))
            v = v + (vf_ref[rows, :] - v) * mix
        kk = k * kkw_ref[...]
        norm = jnp.sqrt(_seg_sum(kk * kk, bd))
        kk_out[rows, :] = (kk / jnp.maximum(norm, 1e-12)).astype(BF16)
        k_out[rows, :] = (k * (1.0 + (a - 1.0) * kaw_ref[...])).astype(BF16)
        r_out[rows, :] = r.astype(BF16)
        v_out[rows, :] = v.astype(BF16)
        a_out[rows, :] = a.astype(BF16)
        cols = slice(c * ac, (c + 1) * ac)
        attn_out[:, cols] = _dot(h, w_ref[:, cols])
    prev_ref[...] = prev


def _inproj_prep(x2d, v_first, prm, B, S, tm, first_layer):
    T, D = x2d.shape
    assert T == B * S and S % tm == 0

    def full(a):
        return pl.BlockSpec(a.shape, lambda i: (0,) * a.ndim)

    row = pl.BlockSpec((tm, RW_WIDTH), lambda i: (i, 0))
    in_specs = [pl.BlockSpec((tm, D), lambda i: (i, 0)), full(prm["norm_g"]),
                pl.BlockSpec(prm["w_proj"].shape, lambda i: (0, 0), pipeline_mode=pl.Buffered(1))]
    args = [x2d, prm["norm_g"], prm["w_proj"]]
    if not first_layer:
        in_specs.append(row)
        args.append(v_first)
    names = ["mu", "w0", "w2", "a0", "a2"]
    if not first_layer:
        names += ["v0", "v2"]
    names += ["g2", "k_k", "k_a", "bd"]
    for n in names:
        in_specs.append(full(prm[n]))
        args.append(prm[n])
    n_feat = 8 if first_layer else 7
    outs = pl.pallas_call(
        functools.partial(_inproj_prep_kernel, tm=tm, tiles_per_seq=S // tm,
                          first_layer=first_layer),
        grid=(T // tm,),
        in_specs=in_specs,
        out_specs=[pl.BlockSpec((tm, COL_RW), lambda i: (i, 0))] + [row] * n_feat,
        out_shape=[jax.ShapeDtypeStruct((T, COL_RW), F32)]
        + [jax.ShapeDtypeStruct((T, RW_WIDTH), dt) for dt in FEATURE_DTYPES[:n_feat]],
        scratch_shapes=[pltpu.VMEM((1, N_PROJ - COL_RW), F32)],
        compiler_params=_cparams("arbitrary"),
        name="inproj_prep",
    )(*args)
    return outs[0], outs[1:]


def _wkv_kernel(r_ref, lw_ref, k_ref, v_ref, kk_ref, a_ref, g_ref, rk_ref, lnw_ref, lnb_ref,
                tri_ref, bd_ref, o_ref, state_ref, *, tb):
    C = CHUNK
    n_pairs = RW_WIDTH // LANES

    @pl.when(pl.program_id(1) == 0)
    def _():
        state_ref[...] = jnp.zeros_like(state_ref)

    lane = lax.broadcasted_iota(jnp.int32, (C, LANES), 1)
    head0 = lane < HEAD_DIM
    rows2 = lax.broadcasted_iota(jnp.int32, (2 * C, 2 * C), 0)
    cols2 = lax.broadcasted_iota(jnp.int32, (2 * C, 2 * C), 1)
    same = (rows2 // C) == (cols2 // C)
    strict = same & (cols2 < rows2)
    incl = same & (cols2 <= rows2)
    eye = (rows2 == cols2).astype(F32)
    blk_xor = rows2 ^ cols2
    tri = tri_ref[...]
    bd = bd_ref[...]

    def stack(x):
        return jnp.concatenate([jnp.where(head0, x, 0.0), jnp.where(head0, 0.0, x)], axis=0)

    def prepare(g, steps=()):
        units = [(dc, p) for dc in range(WKV_GROUP) for p in range(n_pairs)]
        P = range(len(units))
        steps = list(steps)

        def issue_step():
            if steps:
                steps.pop(0)()

        def load(ref):
            return [ref[pl.ds(pl.multiple_of((g * WKV_GROUP + dc) * C, C), C),
                        p * LANES:(p + 1) * LANES] for dc, p in units]

        def load32(ref):
            return [x.astype(F32) for x in load(ref)]

        lw = load(lw_ref)
        r, k, v, kk, a = (load32(x) for x in (r_ref, k_ref, v_ref, kk_ref, a_ref))
        cl = [_split_dot_left(tri, lw[p]) for p in P]
        e_pos = [jnp.exp(cl[p]) for p in P]
        e_neg = [jnp.exp(-cl[p]) for p in P]
        rt = [stack(r[p] * e_pos[p]).astype(BF16) for p in P]
        at = [stack(-kk[p] * jnp.exp(cl[p] - lw[p])).astype(BF16) for p in P]
        bt = [stack(kk[p] * a[p] * e_neg[p]).astype(BF16) for p in P]
        kt = [stack(k[p] * e_neg[p]).astype(BF16) for p in P]
        vs = [stack(v[p]).astype(BF16) for p in P]
        prod = [_dot_nt(jnp.concatenate([at[p], rt[p]], axis=0),
                        jnp.concatenate([bt[p], kt[p]], axis=0)) for p in P]
        l_ab = [jnp.where(strict, prod[p][:2 * C, :2 * C], 0.0) for p in P]
        l_ak = [jnp.where(strict, prod[p][:2 * C, 2 * C:], 0.0).astype(BF16) for p in P]
        q_rb = [jnp.where(incl, prod[p][2 * C:, :2 * C], 0.0).astype(BF16) for p in P]
        q_rk = [jnp.where(incl, prod[p][2 * C:, 2 * C:], 0.0).astype(BF16) for p in P]
        issue_step()
        lakv = [_dot(l_ak[p], vs[p]).astype(BF16) for p in P]
        y0 = [_dot(q_rk[p], vs[p]) for p in P]
        kv = [_dot_tn(vs[p], kt[p]) for p in P]

        pm = [eye + jnp.where(blk_xor == 1, l_ab[p], 0.0) for p in P]
        s = 2
        while s < C:
            level = (blk_xor >= s) & (blk_xor < 2 * s)
            e = [jnp.where(level, l_ab[p], 0.0).astype(BF16) for p in P]
            pb = [pm[p].astype(BF16) for p in P]
            et = [_dot(e[p], pb[p]).astype(BF16) for p in P]
            pm = [pm[p] + _dot(pb[p], et[p]) for p in P]
            s *= 2
            if s in (8, 32):
                issue_step()
        tw = [_dot(pm[p].astype(BF16), jnp.concatenate([at[p], lakv[p]], axis=1)) for p in P]
        while steps:
            issue_step()
        return tuple(
            (jnp.concatenate([tw[p][:, :LANES].astype(BF16), rt[p]], axis=0), tw[p][:, LANES:],
             q_rb[p], y0[p][:C] + y0[p][C:], bt[p], kv[p], e_pos[p][C - 1:C, :]) for p in P)

    def advance(g, prepared):
        P = range(n_pairs)
        cur = {"states": [state_ref[p] for p in P], "outs": []}

        def chunk_step(dc):
            def run():
                w_r, u0, q_rb, y0, bt, kv, decay = zip(*prepared[dc * n_pairs:(dc + 1) * n_pairs])
                states = cur["states"]
                from_state = [_dot_nt(w_r[p], states[p].astype(BF16)) for p in P]
                ub = [(from_state[p][:2 * C] + u0[p]).astype(BF16) for p in P]
                cur["states"] = [(states[p] + _dot_tn(ub[p], bt[p]) + kv[p]) * decay[p] for p in P]
                y2 = [from_state[p][2 * C:] + _dot(q_rb[p], ub[p]) for p in P]
                cur["outs"].append([y0[p] + y2[p][:C] + y2[p][C:] for p in P])
            return run

        def finish():
            for p in P:
                state_ref[p] = cur["states"][p]
            rows = pl.ds(pl.multiple_of(g * (WKV_GROUP * C), WKV_GROUP * C), WKV_GROUP * C)
            y = jnp.concatenate([jnp.concatenate(o, axis=1) for o in cur["outs"]], axis=0)
            mean = _seg_sum(y, bd) * (1.0 / HEAD_DIM)
            d = y - mean
            var = _seg_sum(d * d, bd, split=False) * (1.0 / HEAD_DIM)
            yn = d * lax.rsqrt(var + GN_EPS) * lnw_ref[...] + lnb_ref[...]
            rk = r_ref[rows, :].astype(F32) * k_ref[rows, :].astype(F32) * rk_ref[...]
            bonus = _seg_sum(rk, bd, split=False) * v_ref[rows, :].astype(F32)
            o_ref[rows, :] = (yn + bonus) * g_ref[rows, :].astype(F32)

        return [chunk_step(dc) for dc in range(WKV_GROUP)] + [finish]

    n_groups = tb // (C * WKV_GROUP)

    def body(g, prepared):
        return prepare(g, advance(g - 1, prepared))

    last = lax.fori_loop(1, n_groups, body, prepare(0))
    for step in advance(n_groups - 1, last):
        step()


def _wkv(feats, prm, B, S, tb):
    T = B * S
    nt = S // tb
    r, lw, k, v, kk, a, g = feats
    row = pl.BlockSpec((tb, RW_WIDTH), lambda b, t: (b * nt + t, 0))

    def full(x):
        return pl.BlockSpec(x.shape, lambda b, t: (0,) * x.ndim)

    small = [prm["r_k"], prm["ln_w"], prm["ln_b"], prm["tri_chunk"], prm["bd"]]
    return pl.pallas_call(
        functools.partial(_wkv_kernel, tb=tb),
        grid=(B, nt),
        in_specs=[row] * 7 + [full(x) for x in small],
        out_specs=row,
        out_shape=jax.ShapeDtypeStruct((T, RW_WIDTH), F32),
        scratch_shapes=[pltpu.VMEM((RW_WIDTH // LANES, LANES, LANES), F32)],
        compiler_params=_cparams("parallel", "arbitrary"),
        name="wkv7",
    )(r, lw, k, v, kk, a, g, *small)


def _outffn_kernel(x_ref, sb_ref, ca_ref, rw_ref, sbg_ref, cag_ref, wo_ref, fg_ref, w1_ref, w2_ref,
                   fin_ref, o_ref, h_ref, acc_ref, *, final_norm):
    j = pl.program_id(1)

    @pl.when(j == 0)
    def _():
        sbn = _rms(sb_ref[...], sbg_ref[...]).astype(BF16)
        can = _rms(ca_ref[...], cag_ref[...]).astype(BF16)
        x = x_ref[...]
        x = x + _dot(sbn, wo_ref[:SB_WIDTH, :])
        x = x + _dot(can, wo_ref[SB_WIDTH:SB_WIDTH + CA_WIDTH, :])
        x = x + _dot(rw_ref[...].astype(BF16), wo_ref[SB_WIDTH + CA_WIDTH:, :])
        acc_ref[...] = x
        h_ref[...] = _rms(x, fg_ref[...]).astype(BF16)

    f = jnp.square(jnp.maximum(_dot(h_ref[...], w1_ref[...]), 0.0)).astype(BF16)
    acc_ref[...] += _dot(f, w2_ref[...])

    @pl.when(j == pl.num_programs(1) - 1)
    def _():
        y = acc_ref[...]
        if final_norm:
            y = _rms(y, fin_ref[...])
        o_ref[...] = y


def _outffn(x2d, sb, ca, rw, prm, tm, tf, final_norm):
    T, D = x2d.shape
    Fd = prm["w_ff_in"].shape[1]

    def rows(w):
        return pl.BlockSpec((tm, w), lambda i, j: (i, 0))

    def full(a):
        return pl.BlockSpec(a.shape, lambda i, j: (0,) * a.ndim)

    return pl.pallas_call(
        functools.partial(_outffn_kernel, final_norm=final_norm),
        grid=(T // tm, Fd // tf),
        in_specs=[
            rows(D), rows(SB_WIDTH), rows(CA_WIDTH), rows(RW_WIDTH),
            full(prm["sb_g"]), full(prm["ca_g"]), full(prm["w_out"]), full(prm["ffn_g"]),
            pl.BlockSpec((D, tf), lambda i, j: (0, j)),
            pl.BlockSpec((tf, D), lambda i, j: (j, 0)),
            full(prm["final_g"]),
        ],
        out_specs=rows(D),
        out_shape=jax.ShapeDtypeStruct((T, D), F32),
        scratch_shapes=[pltpu.VMEM((tm, D), BF16), pltpu.VMEM((tm, D), F32)],
        compiler_params=_cparams("parallel", "arbitrary"),
        name="outproj_ffn",
    )(x2d, sb, ca, rw, prm["sb_g"], prm["ca_g"], prm["w_out"], prm["ffn_g"],
      prm["w_ff_in"], prm["w_ff_out"], prm["final_g"])


def _layer_params(l, p, ca_tq):
    w_in = p["w_in"][l]
    D = w_in.shape[0]
    attn = w_in[:, :COL_RW]
    rest = w_in[:, COL_RW:]
    o_w = RW_WIDTH
    o_k = o_w + D_DECAY_LORA
    o_v = o_k + RW_WIDTH
    o_a = o_v + RW_WIDTH
    o_g = o_a + D_AAA_LORA
    r_c, w_c, k_c, v_c = rest[:, :o_w], rest[:, o_w:o_k], rest[:, o_k:o_v], rest[:, o_v:o_a]
    a_c, g_c = rest[:, o_a:o_g], rest[:, o_g:]
    pad_gv = LORA_GV - D_GATE_LORA - D_MV_LORA
    if l == 0:
        vd_c = jnp.zeros((D, D_MV_LORA), w_in.dtype)
    else:
        vd_c = p["w_vmix_down"][l - 1]
    w_proj = jnp.concatenate(
        [attn, r_c, k_c, v_c, w_c, a_c, g_c, vd_c, jnp.zeros((D, pad_gv), w_in.dtype)], axis=1)
    mu = p["rw_mu"][l]
    mu_main = jnp.concatenate([mu[:o_w], mu[o_k:o_v], mu[o_v:o_a]])[None, :]
    mu_lora = jnp.concatenate(
        [mu[o_w:o_k], mu[o_a:o_g], mu[o_g:], jnp.zeros((D_MV_LORA + pad_gv,), F32)])[None, :]

    def padrows(w, before, total):
        return jnp.pad(w, ((before, total - before - w.shape[0]), (0, 0))).astype(BF16)

    prm = {
        "norm_g": p["norm_mix_g"][l][None, :],
        "w_proj": w_proj,
        "mu": jnp.concatenate([mu_main, mu_lora], axis=1),
        "w0": p["rw_w0"][l][None, :],
        "w2": padrows(p["rw_w2"][l], 0, LORA_WA),
        "a0": p["rw_a0"][l][None, :],
        "a2": padrows(p["rw_a2"][l], D_DECAY_LORA, LORA_WA),
        "g2": padrows(p["rw_g2"][l], 0, LORA_GV),
        "k_k": p["rw_k_k"][l][None, :],
        "k_a": p["rw_k_a"][l][None, :],
        "r_k": p["rw_r_k"][l][None, :],
        "ln_w": p["rw_ln_w"][l][None, :],
        "ln_b": p["rw_ln_b"][l][None, :],
        "sb_g": p["sb_out_g"][l][None, :],
        "ca_g": p["ca_out_g"][l][None, :],
        "w_out": p["w_out"][l],
        "ffn_g": p["norm_ffn_g"][l][None, :],
        "w_ff_in": p["w_ff_in"][l],
        "w_ff_out": p["w_ff_out"][l],
        "final_g": p["norm_final_g"][None, :],
    }
    if l > 0:
        prm["v0"] = p["rw_v0"][l - 1][None, :]
        prm["v2"] = padrows(p["rw_v2"][l - 1], D_GATE_LORA, LORA_GV)
    table = p["ca_rel_bias"][l]
    left = CA_LEFT_CHUNKS * CHUNK + ca_tq - REL_CLIP
    right = 4 * ca_tq - left - table.shape[1]
    rel = jnp.pad(table, ((0, 0), (left, right)), mode="edge")
    prm["rel_rows"] = rel.reshape(CA_WIDTH // HEAD_DIM, 1, 4 * ca_tq)
    return prm


def _constants(sb_tq):
    idx = jnp.arange(sb_tq)
    tri_sb = (idx[:, None] > idx[None, :]).astype(BF16)
    c = jnp.arange(CHUNK)
    tri_chunk = (c[:, None] >= c[None, :]).astype(BF16)
    ln = jnp.arange(LANES)
    bd = ((ln[:, None] // HEAD_DIM) == (ln[None, :] // HEAD_DIM)).astype(BF16)
    return tri_sb, tri_chunk, bd


def _forward(x, p, *, tm_proj, sb_tq, ca_tq, tb_wkv, tm_ffn, tf_ffn):
    B, S, D = x.shape
    depth = p["w_in"].shape[0]
    p = dict(p)
    for name in ("w_in", "w_vmix_down", "w_out", "w_ff_in", "w_ff_out"):
        p[name] = p[name].astype(BF16)
    tri_sb, tri_chunk, bd = _constants(sb_tq)
    x2d = x.reshape(B * S, D)
    v_first = None
    for l in range(depth):
        prm = _layer_params(l, p, ca_tq)
        prm["tri_chunk"] = tri_chunk
        prm["bd"] = bd
        proj, feats = _inproj_prep(x2d, v_first, prm, B, S, tm_proj, l == 0)
        sb = _sb_attention(proj, tri_sb, B, S, sb_tq)
        ca = _ca_attention(proj, prm["rel_rows"], B, S, ca_tq)
        if l == 0:
            v_first = feats[7]
        rw = _wkv(feats[:7], prm, B, S, tb_wkv)
        x2d = _outffn(x2d, sb, ca, rw, prm, tm_ffn, tf_ffn, l == depth - 1)
    return x2d.reshape(B, S, D)


def kernel(x, norm_mix_g, w_in, w_vmix_down, sb_out_g, ca_rel_bias, ca_out_g, rw_mu, rw_w0, rw_w2,
           rw_a0, rw_a2, rw_v0, rw_v2, rw_g2, rw_k_k, rw_k_a, rw_r_k, rw_ln_w, rw_ln_b, w_out,
           norm_ffn_g, w_ff_in, w_ff_out, norm_final_g):
    p = dict(norm_mix_g=norm_mix_g, w_in=w_in, w_vmix_down=w_vmix_down, sb_out_g=sb_out_g,
             ca_rel_bias=ca_rel_bias, ca_out_g=ca_out_g, rw_mu=rw_mu, rw_w0=rw_w0, rw_w2=rw_w2,
             rw_a0=rw_a0, rw_a2=rw_a2, rw_v0=rw_v0, rw_v2=rw_v2, rw_g2=rw_g2, rw_k_k=rw_k_k,
             rw_k_a=rw_k_a, rw_r_k=rw_r_k, rw_ln_w=rw_ln_w, rw_ln_b=rw_ln_b, w_out=w_out,
             norm_ffn_g=norm_ffn_g, w_ff_in=w_ff_in, w_ff_out=w_ff_out, norm_final_g=norm_final_g)
    return _forward(x, p, tm_proj=512, sb_tq=256, ca_tq=256, tb_wkv=1024,
                    tm_ffn=1024, tf_ffn=1024)
```
